```python
import math
import jax, jax.numpy as jnp
from jax import lax
import numpy as np

D_MODEL = 1024
BATCH = 8
SEQ = 2048
DEPTH = 2
DEC_BATCH = 128
DEC_SEQ = 8
PAST_LEN = 8192
PAGE_SIZE = 128

N_A_LAYERS = (DEPTH + 1) // 2
N_C_LAYERS = DEPTH // 2

SSD_D_INNER = D_MODEL
SSD_HEAD_DIM = 64
SSD_HEADS = SSD_D_INNER // SSD_HEAD_DIM
SSD_GROUPS = 2
SSD_STATE = 128
SSD_CONV = 4
SSD_CHUNK = 128
SSD_CONV_DIM = SSD_D_INNER + 2 * SSD_GROUPS * SSD_STATE

DIFF_HEAD_DIM = 64
DIFF_HEADS = D_MODEL // (2 * DIFF_HEAD_DIM)
DIFF_KV_HEADS = 4
DIFF_Q_BLOCK = 128

SWA_HEAD_DIM = 64
SWA_HEADS = D_MODEL // SWA_HEAD_DIM
SWA_KV_HEADS = 4
WINDOW = 128

ROPE_THETA = 500000.0
ROPE_FRACTION = 4
D_FF = 4 * D_MODEL
EPS = 1e-6

A_SPLITS = [SSD_D_INNER, SSD_CONV_DIM, SSD_HEADS,
            DIFF_HEADS * 2 * DIFF_HEAD_DIM, DIFF_KV_HEADS * 2 * DIFF_HEAD_DIM, DIFF_KV_HEADS * 2 * DIFF_HEAD_DIM]
A_IN = sum(A_SPLITS)
A_MIX = SSD_D_INNER + DIFF_HEADS * 2 * DIFF_HEAD_DIM
C_SPLITS = [SWA_HEADS * SWA_HEAD_DIM, SWA_KV_HEADS * SWA_HEAD_DIM, SWA_KV_HEADS * SWA_HEAD_DIM]
C_IN = sum(C_SPLITS)

kernel_name = 'hybrid_ssd_diffattn_swa_decode_step'


def _offsets(sizes):
    return np.cumsum(sizes)[:-1].tolist()


def rms_norm(x, g):
    xf = x.astype(jnp.float32)
    y = xf * lax.rsqrt(jnp.mean(xf * xf, axis=-1, keepdims=True) + EPS)
    return (y * g.astype(jnp.float32)).astype(x.dtype)


def partial_rope(x, pos):
    dh = x.shape[-1]
    rot = dh // ROPE_FRACTION
    half = rot // 2
    inv_freq = ROPE_THETA ** (-jnp.arange(half, dtype=jnp.float32) * (2.0 / rot))
    ang = pos.astype(jnp.float32)[:, None] * inv_freq[None, :]
    cos = jnp.cos(ang)[:, None, :]
    sin = jnp.sin(ang)[:, None, :]
    xf = x.astype(jnp.float32)
    x1, x2, rest = xf[..., :half], xf[..., half:rot], xf[..., rot:]
    out = jnp.concatenate([x1 * cos - x2 * sin, x2 * cos + x1 * sin, rest], axis=-1)
    return out.astype(x.dtype)


def causal_dwconv(u, prev, w, b):
    S = u.shape[1]
    full = jnp.concatenate([prev.astype(u.dtype), u], axis=1)
    y = sum(full[:, k:k + S] * w[k] for k in range(SSD_CONV)) + b
    return y, full[:, -(SSD_CONV - 1):]


def ssd_chunked(x, a, Bm, Cm, h0):
    b, s, h, p = x.shape
    g, n = Bm.shape[2], Bm.shape[3]
    L = math.gcd(s, SSD_CHUNK)
    c = s // L
    hg = h // g
    x = x.reshape(b, c, L, g, hg, p)
    a = a.reshape(b, c, L, g, hg).transpose(0, 3, 4, 1, 2)
    Bm = Bm.reshape(b, c, L, g, n)
    Cm = Cm.reshape(b, c, L, g, n)
    a_cum = jnp.cumsum(a, axis=-1)
    tri = jnp.tril(jnp.ones((L, L), dtype=bool))
    Lmat = jnp.exp(jnp.where(tri, a_cum[..., :, None] - a_cum[..., None, :], -jnp.inf))
    y_diag = jnp.einsum('bclgn,bcsgn,bgjcls,bcsgjp->bclgjp', Cm, Bm, Lmat, x)
    decay_states = jnp.exp(a_cum[..., -1:] - a_cum)
    states = jnp.einsum('bclgn,bgjcl,bclgjp->bcgjpn', Bm, decay_states, x)
    states = jnp.concatenate([h0.reshape(b, 1, g, hg, p, n), states], axis=1)
    cs = jnp.cumsum(jnp.pad(a_cum[..., -1], ((0, 0), (0, 0), (0, 0), (1, 0))), axis=-1)
    tri_c = jnp.tril(jnp.ones((c + 1, c + 1), dtype=bool))
    dec = jnp.exp(jnp.where(tri_c, cs[..., :, None] - cs[..., None, :], -jnp.inf))
    new_states = jnp.einsum('bgjzc,bcgjpn->bzgjpn', dec, states)
    prev_states, final = new_states[:, :-1], new_states[:, -1]
    y_off = jnp.einsum('bclgn,bcgjpn,bgjcl->bclgjp', Cm, prev_states, jnp.exp(a_cum))
    y = (y_diag + y_off).reshape(b, s, h, p)
    return y, final.reshape(b, h, p, n)


def diff_attn_dense(q, k, v, pos, lam):
    B, S = q.shape[:2]
    nb = S // DIFF_Q_BLOCK
    scale = DIFF_HEAD_DIM ** -0.5
    qb = jnp.moveaxis(q.reshape((B, nb, DIFF_Q_BLOCK) + q.shape[2:]), 1, 0)
    pb = pos.reshape(nb, DIFF_Q_BLOCK)

    def one_block(args):
        qi, pi = args
        s = jnp.einsum('bqkgjd,bskjd->bkgjqs', qi, k).astype(jnp.float32) * scale
        mask = pi[:, None] >= pos[None, :]
        p = jax.nn.softmax(jnp.where(mask, s, -jnp.inf), axis=-1)
        att = p[:, :, :, 0] - lam * p[:, :, :, 1]
        return jnp.einsum('bkgqs,bskv->bqkgv', att.astype(v.dtype), v)

    o = lax.map(one_block, (qb, pb))
    return jnp.moveaxis(o, 0, 1).reshape(B, S, DIFF_HEADS, v.shape[-1])


def diff_attn_paged(q, k, v, cache_k, cache_v, page_table, j, lam):
    B, Q = q.shape[:2]
    dv = v.shape[-1]
    scale = DIFF_HEAD_DIM ** -0.5

    def update(carry, kb, vb, mask):
        m, l, acc = carry
        s = jnp.einsum('bqkgjd,bskjd->bkgjqs', q, kb).astype(jnp.float32) * scale
        if mask is not None:
            s = jnp.where(mask, s, -jnp.inf)
        m_new = jnp.maximum(m, s.max(axis=-1))
        corr = jnp.exp(m - m_new)
        p = jnp.exp(s - m_new[..., None])
        l = l * corr + p.sum(axis=-1)
        acc = acc * corr[..., None] + jnp.einsum('bkgjqs,bskv->bkgjqv', p, vb.astype(jnp.float32))
        return (m_new, l, acc)

    def step(carry, pages):
        kb = cache_k[j, pages].reshape(B, PAGE_SIZE, DIFF_KV_HEADS, 2, DIFF_HEAD_DIM)
        vb = cache_v[j, pages]
        return update(carry, kb, vb, None), None

    G = DIFF_HEADS // DIFF_KV_HEADS
    init = (jnp.full((B, DIFF_KV_HEADS, G, 2, Q), -jnp.inf, jnp.float32),
            jnp.zeros((B, DIFF_KV_HEADS, G, 2, Q), jnp.float32),
            jnp.zeros((B, DIFF_KV_HEADS, G, 2, Q, dv), jnp.float32))
    carry, _ = lax.scan(step, init, page_table.T)
    causal = jnp.tril(jnp.ones((Q, Q), dtype=bool))
    m, l, acc = update(carry, k, v, causal)
    o = acc / l[..., None]
    att = o[:, :, :, 0] - lam * o[:, :, :, 1]
    return jnp.transpose(att, (0, 3, 1, 2, 4)).reshape(B, Q, DIFF_HEADS, dv).astype(v.dtype)


def mixer_ab(xn, pos, conv_prev, ssm_prev, paged, P, j, layer_idx):
    B, S, _ = xn.shape
    z, xbc, dt_raw, q, k, v = jnp.split(xn @ P['a_w_in'][j], _offsets(A_SPLITS), axis=-1)
    xbc, conv_new = causal_dwconv(xbc, conv_prev, P['a_conv_w'][j], P['a_conv_b'][j])
    xbc = jax.nn.silu(xbc)
    xs, Bm, Cm = jnp.split(xbc.astype(jnp.float32), _offsets([SSD_D_INNER, SSD_GROUPS * SSD_STATE, SSD_GROUPS * SSD_STATE]), axis=-1)
    xs = xs.reshape(B, S, SSD_HEADS, SSD_HEAD_DIM)
    Bm = Bm.reshape(B, S, SSD_GROUPS, SSD_STATE)
    Cm = Cm.reshape(B, S, SSD_GROUPS, SSD_STATE)
    dt = jax.nn.softplus(dt_raw.astype(jnp.float32) + P['a_dt_bias'][j].astype(jnp.float32))
    A = -jnp.exp(P['a_A_log'][j].astype(jnp.float32))
    y, ssm_new = ssd_chunked(xs * dt[..., None], dt * A, Bm, Cm, ssm_prev.astype(jnp.float32))
    y = y + xs * P['a_D'][j].astype(jnp.float32)[:, None]
    gated = (y.reshape(B, S, SSD_D_INNER) * jax.nn.silu(z.astype(jnp.float32))).reshape(B, S, SSD_GROUPS, SSD_D_INNER // SSD_GROUPS)
    ssd_out = rms_norm(gated, P['a_ssd_norm'][j].reshape(SSD_GROUPS, -1)).reshape(B, S, SSD_D_INNER).astype(xn.dtype)
    q = partial_rope(rms_norm(q.reshape(B, S, 2 * DIFF_HEADS, DIFF_HEAD_DIM), P['a_q_norm'][j]), pos)
    k = partial_rope(rms_norm(k.reshape(B, S, 2 * DIFF_KV_HEADS, DIFF_HEAD_DIM), P['a_k_norm'][j]), pos)
    v = v.reshape(B, S, DIFF_KV_HEADS, 2 * DIFF_HEAD_DIM)
    lam_init = 0.8 - 0.6 * math.exp(-0.3 * layer_idx)
    f32 = jnp.float32
    lam = (jnp.exp(jnp.sum(P['a_lam_q1'][j].astype(f32) * P['a_lam_k1'][j].astype(f32)))
           - jnp.exp(jnp.sum(P['a_lam_q2'][j].astype(f32) * P['a_lam_k2'][j].astype(f32))) + lam_init)
    G = DIFF_HEADS // DIFF_KV_HEADS
    qg = q.reshape(B, S, DIFF_KV_HEADS, G, 2, DIFF_HEAD_DIM)
    kg = k.reshape(B, S, DIFF_KV_HEADS, 2, DIFF_HEAD_DIM)
    if paged is None:
        o = diff_attn_dense(qg, kg, v, pos, lam)
    else:
        cache_k, cache_v, page_table = paged
        o = diff_attn_paged(qg, kg, v, cache_k, cache_v, page_table, j, lam)
    o = (rms_norm(o, P['a_subln'][j]) * (1.0 - lam_init)).reshape(B, S, DIFF_HEADS * 2 * DIFF_HEAD_DIM)
    mix = jnp.concatenate([ssd_out, o.astype(xn.dtype)], axis=-1) @ P['a_w_out'][j]
    k_rows = kg.reshape(B, S, DIFF_KV_HEADS, 2 * DIFF_HEAD_DIM)
    return mix, k_rows, v, conv_new, ssm_new


def sink_softmax(s, mask, sink):
    s = jnp.where(mask, s, -jnp.inf)
    m = jnp.maximum(s.max(axis=-1, keepdims=True), sink)
    p = jnp.exp(s - m)
    return p / (p.sum(axis=-1, keepdims=True) + jnp.exp(sink - m))


def mixer_c(xn, pos, win_prev, P, j):
    B, S, _ = xn.shape
    G = SWA_HEADS // SWA_KV_HEADS
    scale = SWA_HEAD_DIM ** -0.5
    q, k, v = jnp.split(xn @ P['c_w_in'][j], _offsets(C_SPLITS), axis=-1)
    q = partial_rope(rms_norm(q.reshape(B, S, SWA_HEADS, SWA_HEAD_DIM), P['c_q_norm'][j]), pos)
    k = partial_rope(rms_norm(k.reshape(B, S, SWA_KV_HEADS, SWA_HEAD_DIM), P['c_k_norm'][j]), pos)
    v = v.reshape(B, S, SWA_KV_HEADS, SWA_HEAD_DIM)
    qg = q.reshape(B, S, SWA_KV_HEADS, G, SWA_HEAD_DIM)
    sink = P['c_sinks'][j].astype(jnp.float32).reshape(SWA_KV_HEADS, G, 1, 1)
    if win_prev is None:
        W = WINDOW
        nb = S // W
        qb = qg.reshape(B, nb, W, SWA_KV_HEADS, G, SWA_HEAD_DIM)
        kc = k.reshape(B, nb, W, SWA_KV_HEADS, SWA_HEAD_DIM)
        vc = v.reshape(B, nb, W, SWA_KV_HEADS, SWA_HEAD_DIM)
        kb = jnp.concatenate([jnp.concatenate([jnp.zeros_like(kc[:, :1]), kc[:, :-1]], axis=1), kc], axis=2)
        vb = jnp.concatenate([jnp.concatenate([jnp.zeros_like(vc[:, :1]), vc[:, :-1]], axis=1), vc], axis=2)
        qpos = pos.reshape(nb, W)
        kpos = jnp.concatenate([qpos - W, qpos], axis=-1)
        diff = qpos[:, :, None] - kpos[:, None, :]
        mask = (diff >= 0) & (diff < WINDOW) & (kpos[:, None, :] >= 0)
        s = jnp.einsum('bnqkgd,bnskd->bnkgqs', qb, kb).astype(jnp.float32) * scale
        p = sink_softmax(s, mask[None, :, None, None], sink)
        o = jnp.einsum('bnkgqs,bnskd->bnqkgd', p.astype(v.dtype), vb).reshape(B, S, SWA_HEADS * SWA_HEAD_DIM)
        keep = min(WINDOW, S)
        new_k, new_v = k[:, -keep:], v[:, -keep:]
    else:
        wk, wv = win_prev
        wb = wk.shape[1]
        kall = jnp.concatenate([wk.astype(k.dtype), k], axis=1)
        vall = jnp.concatenate([wv.astype(v.dtype), v], axis=1)
        kpos = jnp.concatenate([PAST_LEN - wb + jnp.arange(wb, dtype=jnp.int32), pos])
        diff = pos[:, None] - kpos[None, :]
        mask = (diff >= 0) & (diff < WINDOW)
        s = jnp.einsum('bqkgd,bskd->bkgqs', qg, kall).astype(jnp.float32) * scale
        p = sink_softmax(s, mask, sink)
        o = jnp.einsum('bkgqs,bskd->bqkgd', p.astype(v.dtype), vall).reshape(B, S, SWA_HEADS * SWA_HEAD_DIM)
        new_k, new_v = kall[:, -wb:], vall[:, -wb:]
    return o @ P['c_w_out'][j], new_k, new_v


def run_group(x, pos, conv_in, ssm_in, wink_in, winv_in, paged, P):
    h = x
    ks, vs, convs, ssms, wks, wvs = [], [], [], [], [], []
    for i in range(DEPTH):
        j = i // 2
        xn = rms_norm(h, P['norm_mix'][i])
        if i % 2 == 0:
            mix, k_rows, v_rows, conv_new, ssm_new = mixer_ab(xn, pos, conv_in[j], ssm_in[j], paged, P, j, i)
            ks.append(k_rows); vs.append(v_rows); convs.append(conv_new); ssms.append(ssm_new)
        else:
            win = None if wink_in is None else (wink_in[j], winv_in[j])
            mix, wk, wv = mixer_c(xn, pos, win, P, j)
            wks.append(wk); wvs.append(wv)
        h = h + mix
        hn = rms_norm(h, P['norm_mlp'][i])
        h = h + jnp.square(jax.nn.relu(hn @ P['w_up'][i])) @ P['w_down'][i]
    return h, jnp.stack(ks), jnp.stack(vs), jnp.stack(convs), jnp.stack(ssms), jnp.stack(wks), jnp.stack(wvs)


def setup_inputs(seed: int = 0) -> dict:
    key = jax.random.key(seed)
    keys = iter(jax.random.split(key, 48))
    f32 = jnp.float32

    def nrm(shape, scale):
        return jax.random.normal(next(keys), shape, f32) * scale

    def gain(shape):
        return 1.0 + nrm(shape, 0.02)

    n_pages = PAST_LEN // PAGE_SIZE
    n_used = DEC_BATCH * n_pages
    n_pool = n_used + (n_used + 3) // 4
    perm = jax.random.permutation(next(keys), n_pool)
    page_table = perm[:n_used].reshape(DEC_BATCH, n_pages).astype(jnp.int32)
    wbuf = min(WINDOW, PAST_LEN)

    dt0 = jnp.exp(jax.random.uniform(next(keys), (N_A_LAYERS, SSD_HEADS), f32, math.log(1e-3), math.log(1e-1)))
    dt_bias = dt0 + jnp.log(-jnp.expm1(-dt0))
    A_log = jnp.log(jax.random.uniform(next(keys), (N_A_LAYERS, SSD_HEADS), f32, 1.0, 16.0))

    return {
        'x_prompt': nrm((BATCH, SEQ, D_MODEL), 1.0),
        'x_sample': nrm((DEC_BATCH, DEC_SEQ, D_MODEL), 1.0),
        'cache_k': nrm((N_A_LAYERS, n_pool, PAGE_SIZE, DIFF_KV_HEADS, 2 * DIFF_HEAD_DIM), 1.0),
        'cache_v': nrm((N_A_LAYERS, n_pool, PAGE_SIZE, DIFF_KV_HEADS, 2 * DIFF_HEAD_DIM), 1.0),
        'page_table': page_table,
        'state_conv': nrm((N_A_LAYERS, DEC_BATCH, SSD_CONV - 1, SSD_CONV_DIM), 1.0),
        'state_ssm': nrm((N_A_LAYERS, DEC_BATCH, SSD_HEADS, SSD_HEAD_DIM, SSD_STATE), 0.1),
        'cache_win_k': nrm((N_C_LAYERS, DEC_BATCH, wbuf, SWA_KV_HEADS, SWA_HEAD_DIM), 1.0),
        'cache_win_v': nrm((N_C_LAYERS, DEC_BATCH, wbuf, SWA_KV_HEADS, SWA_HEAD_DIM), 1.0),
        'norm_mix': gain((DEPTH, D_MODEL)),
        'norm_mlp': gain((DEPTH, D_MODEL)),
        'w_up': nrm((DEPTH, D_MODEL, D_FF), D_MODEL ** -0.5),
        'w_down': nrm((DEPTH, D_FF, D_MODEL), D_FF ** -0.5),
        'a_w_in': nrm((N_A_LAYERS, D_MODEL, A_IN), D_MODEL ** -0.5),
        'a_conv_w': nrm((N_A_LAYERS, SSD_CONV, SSD_CONV_DIM), SSD_CONV ** -0.5),
        'a_conv_b': nrm((N_A_LAYERS, SSD_CONV_DIM), 0.02),
        'a_dt_bias': dt_bias,
        'a_A_log': A_log,
        'a_D': 1.0 + nrm((N_A_LAYERS, SSD_HEADS), 0.1),
        'a_ssd_norm': gain((N_A_LAYERS, SSD_D_INNER)),
        'a_q_norm': gain((N_A_LAYERS, DIFF_HEAD_DIM)),
        'a_k_norm': gain((N_A_LAYERS, DIFF_HEAD_DIM)),
        'a_lam_q1': nrm((N_A_LAYERS, DIFF_HEAD_DIM), 0.1),
        'a_lam_k1': nrm((N_A_LAYERS, DIFF_HEAD_DIM), 0.1),
        'a_lam_q2': nrm((N_A_LAYERS, DIFF_HEAD_DIM), 0.1),
        'a_lam_k2': nrm((N_A_LAYERS, DIFF_HEAD_DIM), 0.1),
        'a_subln': gain((N_A_LAYERS, 2 * DIFF_HEAD_DIM)),
        'a_w_out': nrm((N_A_LAYERS, A_MIX, D_MODEL), A_MIX ** -0.5),
        'c_w_in': nrm((N_C_LAYERS, D_MODEL, C_IN), D_MODEL ** -0.5),
        'c_q_norm': gain((N_C_LAYERS, SWA_HEAD_DIM)),
        'c_k_norm': gain((N_C_LAYERS, SWA_HEAD_DIM)),
        'c_sinks': nrm((N_C_LAYERS, SWA_HEADS), 0.5),
        'c_w_out': nrm((N_C_LAYERS, SWA_HEADS * SWA_HEAD_DIM, D_MODEL), (SWA_HEADS * SWA_HEAD_DIM) ** -0.5),
    }


def reference(x_prompt, x_sample, cache_k, cache_v, page_table, state_conv, state_ssm, cache_win_k, cache_win_v,
              norm_mix, norm_mlp, w_up, w_down, a_w_in, a_conv_w, a_conv_b, a_dt_bias, a_A_log, a_D, a_ssd_norm,
              a_q_norm, a_k_norm, a_lam_q1, a_lam_k1, a_lam_q2, a_lam_k2, a_subln, a_w_out,
              c_w_in, c_q_norm, c_k_norm, c_sinks, c_w_out):
    P = {'norm_mix': norm_mix, 'norm_mlp': norm_mlp, 'w_up': w_up, 'w_down': w_down,
         'a_w_in': a_w_in, 'a_conv_w': a_conv_w, 'a_conv_b': a_conv_b, 'a_dt_bias': a_dt_bias,
         'a_A_log': a_A_log, 'a_D': a_D, 'a_ssd_norm': a_ssd_norm, 'a_q_norm': a_q_norm, 'a_k_norm': a_k_norm,
         'a_lam_q1': a_lam_q1, 'a_lam_k1': a_lam_k1, 'a_lam_q2': a_lam_q2, 'a_lam_k2': a_lam_k2,
         'a_subln': a_subln, 'a_w_out': a_w_out, 'c_w_in': c_w_in, 'c_q_norm': c_q_norm,
         'c_k_norm': c_k_norm, 'c_sinks': c_sinks, 'c_w_out': c_w_out}
    Bp, Sp = x_prompt.shape[0], x_prompt.shape[1]
    pos_p = jnp.arange(Sp, dtype=jnp.int32)
    conv0 = jnp.zeros((N_A_LAYERS, Bp, SSD_CONV - 1, SSD_CONV_DIM), x_prompt.dtype)
    ssm0 = jnp.zeros((N_A_LAYERS, Bp, SSD_HEADS, SSD_HEAD_DIM, SSD_STATE), jnp.float32)
    y_prompt, p_k, p_v, p_conv, p_ssm, p_wk, p_wv = run_group(x_prompt, pos_p, conv0, ssm0, None, None, None, P)
    pos_s = PAST_LEN + jnp.arange(x_sample.shape[1], dtype=jnp.int32)
    y_sample, s_k, s_v, s_conv, s_ssm, s_wk, s_wv = run_group(
        x_sample, pos_s, state_conv, state_ssm, cache_win_k, cache_win_v, (cache_k, cache_v, page_table), P)
    return (y_prompt, y_sample, p_k, p_v, s_k, s_v, p_conv, s_conv, p_ssm, s_ssm, p_wk, p_wv, s_wk, s_wv)
```

```python
import functools
import math

import jax
import jax.numpy as jnp
from jax import lax
from jax.experimental import pallas as pl
from jax.experimental.pallas import tpu as pltpu

F32 = jnp.float32
BF16 = jnp.bfloat16

SSD_HEAD_DIM = 64
SSD_GROUPS = 2
SSD_STATE = 128
SSD_CONV = 4
SSD_CHUNK = 128
DIFF_HEAD_DIM = 64
DIFF_KV_HEADS = 4
SWA_HEAD_DIM = 64
SWA_KV_HEADS = 4
WINDOW = 128
ROPE_THETA = 500000.0
ROPE_ROT = 16
EPS = 1e-6

LANES = 128
HEAD = 64
VMEM_LIMIT = 56 * 1024 * 1024


def _dot(a, b):
    return jnp.dot(a, b, preferred_element_type=F32)


def _dot_nt(a, b):
    return lax.dot_general(a, b, (((1,), (1,)), ((), ())), preferred_element_type=F32)


def _split2_dot(v, m):
    hi = v.astype(BF16)
    lo = (v - hi.astype(F32)).astype(BF16)
    return _dot(hi, m) + _dot(lo, m)


def _split3_dot(m, v):
    hi = v.astype(BF16)
    r1 = v - hi.astype(F32)
    mid = r1.astype(BF16)
    lo = (r1 - mid.astype(F32)).astype(BF16)
    return _dot(m, hi) + _dot(m, mid) + _dot(m, lo)


def _silu(x):
    return x * (1.0 / (1.0 + jnp.exp(-x)))


def _softplus(x):
    return jnp.maximum(x, 0.0) + jnp.log(1.0 + jnp.exp(-jnp.abs(x)))


def _lane_half(shape):
    return (lax.broadcasted_iota(jnp.int32, shape, len(shape) - 1) % LANES) // HEAD


def _params(sem, vmem=VMEM_LIMIT):
    return pltpu.CompilerParams(dimension_semantics=sem, vmem_limit_bytes=vmem)


def _const_spec(shape):
    nd = len(shape)
    return pl.BlockSpec(shape, lambda *_: (0,) * nd)


def _row_specs(src, tm, n_p):
    if isinstance(src, tuple):
        p, s = src
        wp = p.shape[1]
        return [pl.BlockSpec((tm, wp), lambda i, *_: (jnp.minimum(i, n_p - 1), 0)),
                pl.BlockSpec((tm, wp), lambda i, *_: (jnp.maximum(i - n_p, 0), 0))], [p, s]
    return [pl.BlockSpec((tm, src.shape[1]), lambda i, *_: (i, 0))], [src]


def _row_load(refs, i, n_p, dtype):
    if len(refs) == 2:
        return jnp.where(i < n_p, refs[0][...].astype(dtype), refs[1][...].astype(dtype))
    return refs[0][...].astype(dtype)


def _rope_tables(seq, past_len, dec_seq, tm):
    half = ROPE_ROT // 2
    inv_freq = ROPE_THETA ** (-jnp.arange(half, dtype=F32) * (2.0 / ROPE_ROT))
    pos = jnp.concatenate([jnp.arange(seq, dtype=jnp.int32),
                           past_len + (jnp.arange(tm, dtype=jnp.int32) % dec_seq)]).astype(F32)
    ang = pos[:, None] * inv_freq[None, :]
    cos, sin = jnp.cos(ang), jnp.sin(ang)
    d = jnp.arange(LANES) % HEAD
    cos_l = jnp.where(d < ROPE_ROT, cos[:, d % half], 1.0)
    sin_a = jnp.where(d < half, -sin[:, d % half], 0.0)
    sin_b = jnp.where((d >= half) & (d < ROPE_ROT), sin[:, d % half], 0.0)
    return cos_l.astype(F32), sin_a.astype(F32), sin_b.astype(F32)


def _head_norm_rope(x, gain, bd, cos_l, sin_a, sin_b, scale):
    ms = _dot((x * x).astype(BF16), bd)
    y = x * lax.rsqrt(ms + EPS) * gain
    half = ROPE_ROT // 2
    y = y * cos_l + pltpu.roll(y, LANES - half, 1) * sin_a + pltpu.roll(y, half, 1) * sin_b
    if scale != 1.0:
        y = y * scale
    return y


def _norm_proj_kernel(*refs, n_src, n_p, segs, n_out):
    src = refs[:n_src]
    g_ref, cos_ref, sa_ref, sb_ref, bd_ref = refs[n_src:n_src + 5]
    pos = n_src + 5
    w_refs = refs[pos:pos + len(segs)]
    pos += len(segs)
    n_gain = sum(1 for s in segs if s["kind"] == "qk")
    gain_refs = refs[pos:pos + n_gain]
    pos += n_gain
    out_refs = refs[pos:pos + n_out]

    i = pl.program_id(0)
    x = _row_load(src, i, n_p, F32)
    xn = (x * lax.rsqrt(jnp.mean(x * x, axis=-1, keepdims=True) + EPS) * g_ref[...]).astype(BF16)

    oi = 0
    gi = 0
    for s, w_ref in zip(segs, w_refs):
        width = s["width"]
        outs = out_refs[oi:oi + len(s["outs"])]
        oi += len(s["outs"])
        if s["kind"] == "qk":
            gain = gain_refs[gi][...]
            gi += 1
        step = min(width, 512)
        for c0 in range(0, width, step):
            acc = _dot(xn, w_ref[:, c0:c0 + step])
            if s["kind"] == "plain":
                for o_ref, spec in zip(outs, s["outs"]):
                    o_ref[:, c0:c0 + step] = acc.astype(spec["dtype"])
                continue
            for l0 in range(0, step, LANES):
                y = acc[:, l0:l0 + LANES]
                if s["kind"] == "qk":
                    y = _head_norm_rope(y, gain, bd_ref[...], cos_ref[...], sa_ref[...], sb_ref[...], s["scale"])
                col = c0 + l0
                for o_ref, spec in zip(outs, s["outs"]):
                    if spec.get("dup"):
                        sw = pltpu.roll(y, HEAD, 1)
                        first = _lane_half(y.shape) == 0
                        o_ref[:, 2 * col:2 * col + LANES] = jnp.where(first, y, sw).astype(spec["dtype"])
                        o_ref[:, 2 * col + LANES:2 * col + 2 * LANES] = jnp.where(first, sw, y).astype(spec["dtype"])
                    else:
                        o_ref[:, col:col + LANES] = y.astype(spec["dtype"])


def _norm_proj(src, gain, segs, rope, bd, tm, n_p, n_t):
    d_model = gain.shape[-1]
    row_specs, row_args = _row_specs(src, tm, n_p)
    cos_l, sin_a, sin_b = rope
    n_pb = (cos_l.shape[0] - tm) // tm
    rope_spec = pl.BlockSpec((tm, LANES), lambda i: (jnp.where(i < n_p, i % n_pb, n_pb), 0))
    in_specs = row_specs + [_const_spec((1, d_model)), rope_spec, rope_spec, rope_spec, _const_spec((LANES, LANES))]
    args = row_args + [gain.reshape(1, d_model), cos_l, sin_a, sin_b, bd]
    for s in segs:
        in_specs.append(_const_spec(s["w"].shape))
        args.append(s["w"])
    for s in segs:
        if s["kind"] == "qk":
            in_specs.append(_const_spec((1, LANES)))
            args.append(jnp.tile(s["gain"].astype(F32), LANES // HEAD).reshape(1, LANES))
    out_shapes, out_specs = [], []
    for s in segs:
        for spec in s["outs"]:
            w = s["width"] * (2 if spec.get("dup") else 1)
            out_shapes.append(jax.ShapeDtypeStruct((n_t * tm, w), spec["dtype"]))
            out_specs.append(pl.BlockSpec((tm, w), lambda i: (i, 0)))
    kern = functools.partial(
        _norm_proj_kernel, n_src=len(row_args), n_p=n_p,
        segs=[{k: v for k, v in s.items() if k not in ("w", "gain")} for s in segs], n_out=len(out_shapes))
    return pl.pallas_call(
        kern, grid=(n_t,), in_specs=in_specs, out_specs=out_specs, out_shape=out_shapes,
        compiler_params=_params(("arbitrary",)), name="norm_proj")(*args)


def _conv_rolls(u, prevpad, w4, bias, l_sub):
    rows = lax.broadcasted_iota(jnp.int32, u.shape, 0) % l_sub
    n = u.shape[0]
    y = bias + w4[SSD_CONV - 1:SSD_CONV, :] * u
    for k in range(SSD_CONV - 1):
        j = SSD_CONV - 1 - k
        cur = pltpu.roll(u, j, 0)
        prv = pltpu.roll(prevpad, n + j - 8, 0)
        y = y + w4[k:k + 1, :] * jnp.where(rows >= j, cur, prv)
    return y


def _ssd_block(xs_c, bm, cm, dt_raw, z, dtb, a_log, d_x, nw, eexp, hprev_fn, hstore_fn, l_sub):
    n = xs_c.shape[0]
    n_seq = n // l_sub
    hpg = xs_c.shape[1] // SSD_HEAD_DIM
    row = lax.broadcasted_iota(jnp.int32, (n, n), 0)
    col = lax.broadcasted_iota(jnp.int32, (n, n), 1)
    same = (row // l_sub) == (col // l_sub)
    tri = same & (col <= row)

    dt = _softplus(dt_raw + dtb)
    a = dt * (-jnp.exp(a_log))
    a_cum = _split3_dot(jnp.where(tri, 1.0, 0.0).astype(BF16), a)
    a_tot = _split3_dot(jnp.where(same, 1.0, 0.0).astype(BF16), a)
    a_cum_t = a_cum.T
    e_tot_t = jnp.exp(a_tot.T)

    xdt = xs_c * _split2_dot(dt, eexp)
    bm_b = bm.astype(BF16)
    cm_b = cm.astype(BF16)
    g_cb = _dot_nt(cm_b, bm_b)

    half = _lane_half((n, LANES))
    y_parts = []
    for pr in range(hpg // 2):
        ms = []
        for hh in range(2):
            h = 2 * pr + hh
            diff = a_cum[:, h:h + 1] - a_cum_t[h:h + 1, :]
            ms.append((g_cb * jnp.exp(jnp.where(tri, diff, -jnp.inf))).astype(BF16))
        xp = xdt[:, pr * LANES:(pr + 1) * LANES]
        xcat = jnp.concatenate([jnp.where(half == 0, xp, 0.0), jnp.where(half == 1, xp, 0.0)], axis=0).astype(BF16)
        y_parts.append(_dot(jnp.concatenate(ms, axis=1), xcat))
    y = jnp.concatenate(y_parts, axis=1)

    xw_t = (xdt * _split2_dot(jnp.exp(a_tot - a_cum), eexp)).T.astype(BF16)
    seq_row = lax.broadcasted_iota(jnp.int32, (n, 1), 0) // l_sub
    y_off = jnp.zeros_like(y)
    for b in range(n_seq):
        h_b = hprev_fn(b)
        r_b = _dot_nt(cm_b, h_b.reshape(hpg * SSD_HEAD_DIM, SSD_STATE).astype(BF16))
        bm_sel = bm_b if n_seq == 1 else jnp.where(seq_row == b, bm, 0.0).astype(BF16)
        s_b = _dot(xw_t, bm_sel)
        y_off = r_b if n_seq == 1 else jnp.where(seq_row == b, r_b, y_off)
        t0 = b * l_sub
        for h in range(hpg):
            hstore_fn(b, h, e_tot_t[h:h + 1, t0:t0 + 1] * h_b[h] + s_b[h * SSD_HEAD_DIM:(h + 1) * SSD_HEAD_DIM, :])
    y = y + y_off * _split2_dot(jnp.exp(a_cum), eexp) + xs_c * d_x

    gated = y * _silu(z.astype(F32))
    return (gated * lax.rsqrt(jnp.mean(gated * gated, axis=-1, keepdims=True) + EPS) * nw).astype(BF16)


def _ssd_prompt_kernel(xs_ref, b_ref, c_ref, hxs_ref, hb_ref, hc_ref, dt_ref, z_ref,
                       wxs_ref, wb_ref, wc_ref, bxs_ref, bb_ref, bc_ref,
                       dtb_ref, alog_ref, dx_ref, nw_ref, eexp_ref, y_ref, st_ref):
    c = pl.program_id(2)

    @pl.when(c == 0)
    def _():
        st_ref[...] = jnp.zeros_like(st_ref)

    keep = jnp.where(c > 0, 1.0, 0.0)

    def conv(u_ref, h_ref, w_ref, bias_ref):
        u = u_ref[...]
        prevpad = jnp.concatenate([h_ref[...] * keep, jnp.zeros((u.shape[0] - 8, u.shape[1]), F32)], axis=0)
        return _silu(_conv_rolls(u, prevpad, w_ref[...], bias_ref[...], u.shape[0]))

    xs_c = conv(xs_ref, hxs_ref, wxs_ref, bxs_ref)
    bm = conv(b_ref, hb_ref, wb_ref, bb_ref)
    cm = conv(c_ref, hc_ref, wc_ref, bc_ref)

    def hprev(b):
        return st_ref[0]

    def hstore(b, h, val):
        st_ref[0, h] = val

    y_ref[...] = _ssd_block(xs_c, bm, cm, dt_ref[...], z_ref[...], dtb_ref[...], alog_ref[...], dx_ref[...],
                            nw_ref[...], eexp_ref[...], hprev, hstore, xs_c.shape[0])


def _ssd_sample_kernel(xs_ref, b_ref, c_ref, pxs_ref, pb_ref, pc_ref, dt_ref, z_ref,
                       wxs_ref, wb_ref, wc_ref, bxs_ref, bb_ref, bc_ref,
                       dtb_ref, alog_ref, dx_ref, nw_ref, eexp_ref, st_in_ref, y_ref, st_ref, *, l_sub):
    def conv(u_ref, p_ref, w_ref, bias_ref):
        return _silu(_conv_rolls(u_ref[...], p_ref[...], w_ref[...], bias_ref[...], l_sub))

    xs_c = conv(xs_ref, pxs_ref, wxs_ref, bxs_ref)
    bm = conv(b_ref, pb_ref, wb_ref, bb_ref)
    cm = conv(c_ref, pc_ref, wc_ref, bc_ref)

    def hprev(b):
        return st_in_ref[b]

    def hstore(b, h, val):
        st_ref[b, h] = val

    y_ref[...] = _ssd_block(xs_c, bm, cm, dt_ref[...], z_ref[...], dtb_ref[...], alog_ref[...], dx_ref[...],
                            nw_ref[...], eexp_ref[...], hprev, hstore, l_sub)


def _ssd_common_specs(xw, gw, sw, row_of, ng):
    nb = xw // LANES
    def rs(w, colf):
        return pl.BlockSpec((SSD_CHUNK, w), lambda *g: (row_of(*g), colf(g[1])))
    return dict(
        xs=rs(gw, lambda g: g), b=rs(sw, lambda g: nb + g), c=rs(sw, lambda g: nb + ng + g),
        wxs=pl.BlockSpec((SSD_CONV, gw), lambda *g: (0, g[1])),
        wb=pl.BlockSpec((SSD_CONV, sw), lambda *g: (0, nb + g[1])),
        wc=pl.BlockSpec((SSD_CONV, sw), lambda *g: (0, nb + ng + g[1])),
        bxs=pl.BlockSpec((1, gw), lambda *g: (0, g[1])),
        bb=pl.BlockSpec((1, sw), lambda *g: (0, nb + g[1])),
        bc=pl.BlockSpec((1, sw), lambda *g: (0, nb + ng + g[1])),
        lane=pl.BlockSpec((1, LANES), lambda *g: (0, g[1])),
        grp=pl.BlockSpec((1, gw), lambda *g: (0, g[1])),
        eexp=_const_spec((LANES, gw)),
    )


def _ssd_prompt(xbc, dt, z, conv_w, conv_b, dtb, alog, d_x, nw, eexp, batch, seq):
    ng = SSD_GROUPS
    xw = z.shape[1]
    gw, sw = xw // ng, SSD_STATE
    hpg = gw // SSD_HEAD_DIM
    nc = seq // SSD_CHUNK
    row_of = lambda b, g, c: b * nc + c
    sp = _ssd_common_specs(xw, gw, sw, row_of, ng)
    nb = xw // LANES
    sub = SSD_CHUNK // 8

    def halo(w, colf):
        return pl.BlockSpec((8, w), lambda b, g, c: (jnp.maximum((b * nc + c) * sub - 1, 0), colf(g)))

    in_specs = [sp["xs"], sp["b"], sp["c"],
                halo(gw, lambda g: g), halo(sw, lambda g: nb + g), halo(sw, lambda g: nb + ng + g),
                pl.BlockSpec((SSD_CHUNK, LANES), lambda b, g, c: (row_of(b, g, c), g)),
                pl.BlockSpec((SSD_CHUNK, gw), lambda b, g, c: (row_of(b, g, c), g)),
                sp["wxs"], sp["wb"], sp["wc"], sp["bxs"], sp["bb"], sp["bc"],
                sp["lane"], sp["lane"], sp["grp"], sp["grp"], sp["eexp"]]
    out_specs = [pl.BlockSpec((SSD_CHUNK, gw), lambda b, g, c: (row_of(b, g, c), g)),
                 pl.BlockSpec((1, hpg, SSD_HEAD_DIM, SSD_STATE), lambda b, g, c: (b, g, 0, 0))]
    out_shape = [jax.ShapeDtypeStruct((batch * seq, xw), BF16),
                 jax.ShapeDtypeStruct((batch, ng * hpg, SSD_HEAD_DIM, SSD_STATE), F32)]
    return pl.pallas_call(
        _ssd_prompt_kernel, grid=(batch, ng, nc), in_specs=in_specs, out_specs=out_specs, out_shape=out_shape,
        compiler_params=_params(("arbitrary", "arbitrary", "arbitrary")), name="ssd_prompt")(
            xbc, xbc, xbc, xbc, xbc, xbc, dt, z, conv_w, conv_w, conv_w, conv_b, conv_b, conv_b,
            dtb, alog, d_x, nw, eexp)


def _ssd_sample(xbc, prev8, dt, z, conv_w, conv_b, dtb, alog, d_x, nw, eexp, state, t_p, dec_batch, dec_seq):
    ng = SSD_GROUPS
    xw = z.shape[1]
    gw, sw = xw // ng, SSD_STATE
    hpg = gw // SSD_HEAD_DIM
    n_seq = SSD_CHUNK // dec_seq
    nblk = dec_batch // n_seq
    rb0 = t_p // SSD_CHUNK
    row_of = lambda i, g: rb0 + i
    sp = _ssd_common_specs(xw, gw, sw, row_of, ng)
    nb = xw // LANES

    def prev(w, colf):
        return pl.BlockSpec((SSD_CHUNK, w), lambda i, g: (i, colf(g)))

    st_spec = pl.BlockSpec((n_seq, hpg, SSD_HEAD_DIM, SSD_STATE), lambda i, g: (i, g, 0, 0))
    in_specs = [sp["xs"], sp["b"], sp["c"],
                prev(gw, lambda g: g), prev(sw, lambda g: nb + g), prev(sw, lambda g: nb + ng + g),
                pl.BlockSpec((SSD_CHUNK, LANES), lambda i, g: (rb0 + i, g)),
                pl.BlockSpec((SSD_CHUNK, gw), lambda i, g: (rb0 + i, g)),
                sp["wxs"], sp["wb"], sp["wc"], sp["bxs"], sp["bb"], sp["bc"],
                sp["lane"], sp["lane"], sp["grp"], sp["grp"], sp["eexp"], st_spec]
    out_specs = [pl.BlockSpec((SSD_CHUNK, gw), lambda i, g: (i, g)), st_spec]
    out_shape = [jax.ShapeDtypeStruct((dec_batch * dec_seq, xw), BF16),
                 jax.ShapeDtypeStruct(state.shape, F32)]
    return pl.pallas_call(
        functools.partial(_ssd_sample_kernel, l_sub=dec_seq), grid=(nblk, ng), in_specs=in_specs,
        out_specs=out_specs, out_shape=out_shape,
        compiler_params=_params(("arbitrary", "arbitrary")), name="ssd_sample")(
            xbc, xbc, xbc, prev8, prev8, prev8, dt, z, conv_w, conv_w, conv_w, conv_b, conv_b, conv_b,
            dtb, alog, d_x, nw, eexp, state)


def _lambda(lq1, lk1, lq2, lk2, lam_init):
    s1 = jnp.sum(lq1[...] * lk1[...], axis=-1, keepdims=True)
    s2 = jnp.sum(lq2[...] * lk2[...], axis=-1, keepdims=True)
    return jnp.exp(s1) - jnp.exp(s2) + lam_init


def _subln(att, w, lam_init):
    return att * lax.rsqrt(jnp.mean(att * att, axis=-1, keepdims=True) + EPS) * w * (1.0 - lam_init)


def _diff_prompt_kernel(q_ref, k_ref, v_ref, lq1, lk1, lq2, lk2, sub_ref, o_ref,
                        qs_ref, m_ref, l_ref, acc_ref, *, tq, lam_init):
    qb = pl.program_id(2)
    n_g = q_ref.shape[1] // LANES
    rows = n_g * tq
    half = _lane_half((tq, LANES))
    for g in range(n_g):
        qg = q_ref[:, g * LANES:(g + 1) * LANES]
        for j in range(2):
            qs_ref[j, g * tq:(g + 1) * tq, :] = jnp.where(half == j, qg, jnp.zeros_like(qg))
    m_ref[...] = jnp.full(m_ref.shape, -jnp.inf, F32)
    l_ref[...] = jnp.zeros(l_ref.shape, F32)
    acc_ref[...] = jnp.zeros(acc_ref.shape, F32)

    def step(kb, masked):
        start = pl.multiple_of(kb * tq, tq)
        k = k_ref[pl.ds(start, tq), :]
        v = v_ref[pl.ds(start, tq), :]
        for j in range(2):
            s = _dot_nt(qs_ref[j], k)
            if masked:
                qi = lax.broadcasted_iota(jnp.int32, (rows, tq), 0) % tq
                ki = lax.broadcasted_iota(jnp.int32, (rows, tq), 1)
                s = jnp.where(ki <= qi, s, -jnp.inf)
            m_old = m_ref[j]
            m_new = jnp.maximum(m_old, jnp.max(s, axis=-1, keepdims=True))
            corr = jnp.exp(m_old - m_new)
            p = jnp.exp(s - m_new)
            l_ref[j] = l_ref[j] * corr + jnp.sum(p, axis=-1, keepdims=True)
            acc_ref[j] = acc_ref[j] * corr + _dot(p.astype(BF16), v)
            m_ref[j] = m_new

    lax.fori_loop(0, qb, lambda kb, c: (step(kb, False), c)[1], 0)
    step(qb, True)

    lam = _lambda(lq1, lk1, lq2, lk2, lam_init)
    att = acc_ref[0] / l_ref[0] - lam * (acc_ref[1] / l_ref[1])
    res = _subln(att, sub_ref[...], lam_init)
    for g in range(n_g):
        o_ref[:, g * LANES:(g + 1) * LANES] = res[g * tq:(g + 1) * tq].astype(o_ref.dtype)


def _diff_prompt(q, kb, vb, lam_vecs, subln, batch, seq, lam_init):
    t_p = batch * seq
    n_kvh = DIFF_KV_HEADS
    qw = q.shape[1] // n_kvh
    kw = kb.shape[1] // n_kvh
    tq = min(256, seq)
    nq = seq // tq
    rows = (qw // LANES) * tq
    vec = _const_spec((1, DIFF_HEAD_DIM))
    in_specs = [pl.BlockSpec((tq, qw), lambda b, h, i: (b * nq + i, h)),
                pl.BlockSpec((seq, kw), lambda b, h, i: (b, h)),
                pl.BlockSpec((seq, kw), lambda b, h, i: (b, h)),
                vec, vec, vec, vec, _const_spec((1, LANES))]
    return pl.pallas_call(
        functools.partial(_diff_prompt_kernel, tq=tq, lam_init=lam_init),
        grid=(batch, n_kvh, nq), in_specs=in_specs,
        out_specs=pl.BlockSpec((tq, qw), lambda b, h, i: (b * nq + i, h)),
        out_shape=jax.ShapeDtypeStruct((t_p, q.shape[1]), BF16),
        scratch_shapes=[pltpu.VMEM((2, rows, LANES), BF16), pltpu.VMEM((2, rows, 1), F32),
                        pltpu.VMEM((2, rows, 1), F32), pltpu.VMEM((2, rows, LANES), F32)],
        compiler_params=_params(("arbitrary", "arbitrary", "arbitrary")), name="diff_attn_prompt")(
            q, kb, vb, *lam_vecs, subln)


def _diff_paged_kernel(pt_ref, q_ref, kn_ref, vn_ref, *rest, n_pg, dec_seq, lam_init):
    k_pages = rest[:n_pg]
    v_pages = rest[n_pg:2 * n_pg]
    lq1, lk1, lq2, lk2, sub_ref, o_ref, qbd_ref, m_ref, l_ref, acc_ref = rest[2 * n_pg:]
    p = pl.program_id(1)
    n_kvh = DIFF_KV_HEADS
    n_g = q_ref.shape[1] // (n_kvh * LANES)
    rows = n_g * 2 * dec_seq
    page = k_pages[0].shape[0] // DIFF_KV_HEADS

    @pl.when(p == 0)
    def _():
        half = _lane_half((dec_seq, LANES))
        for h in range(n_kvh):
            parts = []
            for g in range(n_g):
                qg = q_ref[:, (h * n_g + g) * LANES:(h * n_g + g + 1) * LANES]
                for j in range(2):
                    parts.append(jnp.where(half == j, qg, 0.0))
            qbd_ref[h] = jnp.concatenate(parts, axis=0).astype(BF16)
        m_ref[...] = jnp.full(m_ref.shape, -jnp.inf, F32)
        l_ref[...] = jnp.zeros(l_ref.shape, F32)
        acc_ref[...] = jnp.zeros(acc_ref.shape, F32)

    def update(h, s, vals):
        m_old = m_ref[h]
        m_new = jnp.maximum(m_old, jnp.max(s, axis=-1, keepdims=True))
        corr = jnp.exp(m_old - m_new)
        pr = jnp.exp(s - m_new)
        l_ref[h] = l_ref[h] * corr + jnp.sum(pr, axis=-1, keepdims=True)
        pv = None
        for i, v in enumerate(vals):
            t = _dot(pr[:, i * page:(i + 1) * page].astype(BF16), v)
            pv = t if pv is None else pv + t
        acc_ref[h] = acc_ref[h] * corr + pv
        m_ref[h] = m_new

    for h in range(n_kvh):
        rows_h = pl.ds(h, page, stride=n_kvh)
        s = jnp.concatenate([_dot_nt(qbd_ref[h], kp[rows_h, :].astype(BF16)) for kp in k_pages], axis=1)
        update(h, s, [vp[rows_h, :].astype(BF16) for vp in v_pages])

    @pl.when(p == pl.num_programs(1) - 1)
    def _():
        lam = _lambda(lq1, lk1, lq2, lk2, lam_init)
        pad = jnp.zeros((page - dec_seq, LANES), F32)
        qi = lax.broadcasted_iota(jnp.int32, (rows, page), 0) % dec_seq
        ki = lax.broadcasted_iota(jnp.int32, (rows, page), 1)
        for h in range(n_kvh):
            cols = slice(h * LANES, (h + 1) * LANES)
            kn = jnp.concatenate([kn_ref[:, cols], pad], axis=0).astype(BF16)
            vn = jnp.concatenate([vn_ref[:, cols], pad], axis=0).astype(BF16)
            s = jnp.where(ki <= qi, _dot_nt(qbd_ref[h], kn), -jnp.inf)
            update(h, s, [vn])
            o = acc_ref[h] / l_ref[h]
            for g in range(n_g):
                r0 = g * 2 * dec_seq
                att = o[r0:r0 + dec_seq] - lam * o[r0 + dec_seq:r0 + 2 * dec_seq]
                c0 = (h * n_g + g) * LANES
                o_ref[:, c0:c0 + LANES] = _subln(att, sub_ref[...], lam_init).astype(o_ref.dtype)


def _diff_paged(q_s, k_all, v_all, cache_k, cache_v, page_table, lam_vecs, subln, t_p, dec_batch, dec_seq, lam_init):
    n_pages = page_table.shape[1]
    n_pg = math.gcd(n_pages, 8)
    n_kvh = DIFF_KV_HEADS
    kw = k_all.shape[1]
    page_rows = cache_k.shape[1]
    n_g = q_s.shape[1] // (n_kvh * LANES)
    rows = n_g * 2 * dec_seq
    rb0 = t_p // dec_seq
    vec = _const_spec((1, DIFF_HEAD_DIM))

    def page_spec(i):
        return pl.BlockSpec((None, page_rows, LANES), lambda b, p, pt: (pt[b, p * n_pg + i], 0, 0))

    in_specs = ([pl.BlockSpec((dec_seq, q_s.shape[1]), lambda b, p, pt: (b, 0)),
                 pl.BlockSpec((dec_seq, kw), lambda b, p, pt: (rb0 + b, 0)),
                 pl.BlockSpec((dec_seq, kw), lambda b, p, pt: (rb0 + b, 0))]
                + [page_spec(i) for i in range(n_pg)] + [page_spec(i) for i in range(n_pg)]
                + [vec, vec, vec, vec, _const_spec((1, LANES))])
    grid_spec = pltpu.PrefetchScalarGridSpec(
        num_scalar_prefetch=1, grid=(dec_batch, n_pages // n_pg), in_specs=in_specs,
        out_specs=pl.BlockSpec((dec_seq, q_s.shape[1]), lambda b, p, pt: (b, 0)),
        scratch_shapes=[pltpu.VMEM((n_kvh, rows, LANES), BF16), pltpu.VMEM((n_kvh, rows, 1), F32),
                        pltpu.VMEM((n_kvh, rows, 1), F32), pltpu.VMEM((n_kvh, rows, LANES), F32)])
    return pl.pallas_call(
        functools.partial(_diff_paged_kernel, n_pg=n_pg, dec_seq=dec_seq, lam_init=lam_init),
        grid_spec=grid_spec, out_shape=jax.ShapeDtypeStruct(q_s.shape, F32),
        compiler_params=_params(("arbitrary", "arbitrary")), name="diff_attn_paged")(
            page_table, q_s, k_all, v_all, *([cache_k] * n_pg), *([cache_v] * n_pg), *lam_vecs, subln)


def _sink_attend(lhs, blocks, sink_col):
    ss = [jnp.where(mask, _dot_nt(lhs, k2), -jnp.inf) for k2, _, mask in blocks]
    m = sink_col
    for s in ss:
        m = jnp.maximum(m, jnp.max(s, axis=-1, keepdims=True))
    den = jnp.exp(sink_col - m)
    o = None
    for s, (_, v2, _) in zip(ss, blocks):
        p = jnp.exp(s - m)
        den = den + jnp.sum(p, axis=-1, keepdims=True)
        t = _dot(p.astype(BF16), v2)
        o = t if o is None else o + t
    return o / den


def _swa_prompt_kernel(q_ref, kc_ref, kp_ref, vc_ref, vp_ref, sink_ref, o_ref):
    n = pl.program_id(1)
    w = q_ref.shape[0]
    n_kvh = SWA_KV_HEADS
    n_g = q_ref.shape[1] // (n_kvh * HEAD)
    half = _lane_half((w, LANES))
    qi = lax.broadcasted_iota(jnp.int32, (n_g * w, 2 * w), 0) % w
    ci = lax.broadcasted_iota(jnp.int32, (n_g * w, 2 * w), 1)
    mask = (ci > qi) & (ci <= qi + w) & ((ci >= w) | (n > 0))
    for h in range(n_kvh):
        cols = slice(h * LANES, (h + 1) * LANES)
        k2 = jnp.concatenate([kp_ref[:, cols], kc_ref[:, cols]], axis=0)
        v2 = jnp.concatenate([vp_ref[:, cols], vc_ref[:, cols]], axis=0)
        parts, sinks = [], []
        for g in range(n_g):
            hd = h * n_g + g
            qg = q_ref[:, (hd // 2) * LANES:(hd // 2 + 1) * LANES]
            parts.append(jnp.where(half == hd % 2, qg, jnp.zeros_like(qg)))
            sinks.append(jnp.broadcast_to(sink_ref[0:1, hd:hd + 1], (w, 1)))
        o2 = _sink_attend(jnp.concatenate(parts, axis=0), [(k2, v2, mask)], jnp.concatenate(sinks, axis=0))
        for a in range(n_g // 2):
            hd = h * n_g + 2 * a
            pair = jnp.where(half == 0, o2[2 * a * w:(2 * a + 1) * w], o2[(2 * a + 1) * w:(2 * a + 2) * w])
            o_ref[:, (hd // 2) * LANES:(hd // 2 + 1) * LANES] = pair.astype(o_ref.dtype)


def _swa_prompt(q, k2, v2, sinks, batch, seq):
    w = WINDOW
    nb = seq // w
    kw = k2.shape[1]
    cur = pl.BlockSpec((w, kw), lambda b, n: (b * nb + n, 0))
    prv = pl.BlockSpec((w, kw), lambda b, n: (jnp.maximum(b * nb + n - 1, 0), 0))
    in_specs = [pl.BlockSpec((w, q.shape[1]), lambda b, n: (b * nb + n, 0)), cur, prv, cur, prv,
                _const_spec(sinks.shape)]
    return pl.pallas_call(
        _swa_prompt_kernel, grid=(batch, nb), in_specs=in_specs,
        out_specs=pl.BlockSpec((w, q.shape[1]), lambda b, n: (b * nb + n, 0)),
        out_shape=jax.ShapeDtypeStruct((batch * seq, q.shape[1]), BF16),
        compiler_params=_params(("arbitrary", "arbitrary")), name="swa_prompt")(q, k2, k2, v2, v2, sinks)


def _swa_sample_kernel(q_ref, wk_ref, wv_ref, kn_ref, vn_ref, sink_ref, o_ref, *, dec_seq):
    nbb = wk_ref.shape[0]
    wb = wk_ref.shape[1]
    n_kvh = SWA_KV_HEADS
    n_g = q_ref.shape[1] // (n_kvh * HEAD)
    rows = n_g * dec_seq
    half8 = _lane_half((dec_seq, LANES))
    half_k = _lane_half((wb, LANES))
    qi = lax.broadcasted_iota(jnp.int32, (rows, wb), 0) % dec_seq
    ci = lax.broadcasted_iota(jnp.int32, (rows, wb), 1)
    mask_c = ci > qi
    mask_n = ci <= qi
    pad = jnp.zeros((wb - dec_seq, LANES), F32)

    def dup(x, kk):
        return jnp.where(half_k == kk, x, pltpu.roll(x, HEAD, 1)).astype(BF16)

    def body(bb, carry):
        r0 = pl.multiple_of(bb * dec_seq, dec_seq)
        q8 = q_ref[pl.ds(r0, dec_seq), :]
        for pr in range(n_kvh // 2):
            cols = slice(pr * LANES, (pr + 1) * LANES)
            kc, vc = wk_ref[bb, :, cols], wv_ref[bb, :, cols]
            kn = jnp.concatenate([kn_ref[pl.ds(r0, dec_seq), cols], pad], axis=0)
            vn = jnp.concatenate([vn_ref[pl.ds(r0, dec_seq), cols], pad], axis=0)
            for kk in range(2):
                h = 2 * pr + kk
                parts, sinks = [], []
                for g in range(n_g):
                    hd = h * n_g + g
                    qg = q8[:, (hd // 2) * LANES:(hd // 2 + 1) * LANES]
                    parts.append(jnp.where(half8 == hd % 2, qg, 0.0))
                    sinks.append(jnp.broadcast_to(sink_ref[0:1, hd:hd + 1], (dec_seq, 1)))
                lhs = jnp.concatenate(parts, axis=0).astype(BF16)
                o2 = _sink_attend(lhs, [(dup(kc, kk), dup(vc, kk), mask_c), (dup(kn, kk), dup(vn, kk), mask_n)],
                                  jnp.concatenate(sinks, axis=0))
                for a in range(n_g // 2):
                    hd = h * n_g + 2 * a
                    pair = jnp.where(half8 == 0, o2[2 * a * dec_seq:(2 * a + 1) * dec_seq],
                                     o2[(2 * a + 1) * dec_seq:(2 * a + 2) * dec_seq])
                    o_ref[pl.ds(r0, dec_seq), (hd // 2) * LANES:(hd // 2 + 1) * LANES] = pair
        return carry

    lax.fori_loop(0, nbb, body, 0)


def _swa_sample(q_s, wk, wv, k_all, v_all, sinks, t_p, dec_batch, dec_seq):
    nbb = math.gcd(dec_batch, 8)
    rows = nbb * dec_seq
    rb0 = t_p // rows
    kw = k_all.shape[1]
    wb = wk.shape[1]
    in_specs = [pl.BlockSpec((rows, q_s.shape[1]), lambda i: (i, 0)),
                pl.BlockSpec((nbb, wb, kw), lambda i: (i, 0, 0)),
                pl.BlockSpec((nbb, wb, kw), lambda i: (i, 0, 0)),
                pl.BlockSpec((rows, kw), lambda i: (rb0 + i, 0)),
                pl.BlockSpec((rows, kw), lambda i: (rb0 + i, 0)),
                _const_spec(sinks.shape)]
    return pl.pallas_call(
        functools.partial(_swa_sample_kernel, dec_seq=dec_seq), grid=(dec_batch // nbb,), in_specs=in_specs,
        out_specs=pl.BlockSpec((rows, q_s.shape[1]), lambda i: (i, 0)),
        out_shape=jax.ShapeDtypeStruct(q_s.shape, F32),
        compiler_params=_params(("arbitrary",)), name="swa_sample")(q_s, wk, wv, k_all, v_all, sinks)


def _out_mlp_kernel(*refs, n_h, n_act, n_p, split_out):
    h_refs = refs[:n_h]
    pos = n_h
    act_refs = []
    for na in n_act:
        act_refs.append(refs[pos:pos + na])
        pos += na
    wo_ref, g_ref, wu_ref, wd_ref = refs[pos:pos + 4]
    pos += 4
    n_o = 2 if split_out else 1
    out_refs = refs[pos:pos + n_o]
    h1_ref, hn_ref, acc_ref = refs[pos + n_o:]
    i = pl.program_id(0)
    j = pl.program_id(1)
    d = h1_ref.shape[1]

    @pl.when(j == 0)
    def _():
        mix = None
        for s, ar in enumerate(act_refs):
            t = _dot(_row_load(ar, i, n_p, BF16), wo_ref[s * d:(s + 1) * d, :])
            mix = t if mix is None else mix + t
        h1 = _row_load(h_refs, i, n_p, F32) + mix
        h1_ref[...] = h1
        hn_ref[...] = (h1 * lax.rsqrt(jnp.mean(h1 * h1, axis=-1, keepdims=True) + EPS) * g_ref[...]).astype(BF16)
        acc_ref[...] = jnp.zeros_like(acc_ref)

    u = jnp.maximum(_dot(hn_ref[...], wu_ref[...]), 0.0)
    acc_ref[...] += _dot((u * u).astype(BF16), wd_ref[...])

    @pl.when(j == pl.num_programs(1) - 1)
    def _():
        res = h1_ref[...] + acc_ref[...]
        if split_out:
            @pl.when(i < n_p)
            def _():
                out_refs[0][...] = res

            @pl.when(i >= n_p)
            def _():
                out_refs[1][...] = res
        else:
            out_refs[0][...] = res


def _out_mlp(h_src, act_srcs, w_out, gain, w_up, w_down, tm, n_p, n_t, split_out):
    d = gain.shape[-1]
    d_ff = w_up.shape[1]
    tf = min(1024, d_ff)
    h_specs, h_args = _row_specs(h_src, tm, n_p)
    in_specs, args, n_act = list(h_specs), list(h_args), []
    for a in act_srcs:
        sp, ar = _row_specs(a, tm, n_p)
        in_specs += sp
        args += ar
        n_act.append(len(ar))
    in_specs += [_const_spec(w_out.shape), _const_spec((1, d)),
                 pl.BlockSpec((d, tf), lambda i, j: (0, j)), pl.BlockSpec((tf, d), lambda i, j: (j, 0))]
    args += [w_out, gain.reshape(1, d), w_up, w_down]
    if split_out:
        n_s = n_t - n_p
        out_shape = [jax.ShapeDtypeStruct((n_p * tm, d), F32), jax.ShapeDtypeStruct((n_s * tm, d), F32)]
        out_specs = [pl.BlockSpec((tm, d), lambda i, j: (jnp.minimum(i, n_p - 1), 0)),
                     pl.BlockSpec((tm, d), lambda i, j: (jnp.maximum(i - n_p, 0), 0))]
    else:
        out_shape = [jax.ShapeDtypeStruct((n_t * tm, d), F32)]
        out_specs = [pl.BlockSpec((tm, d), lambda i, j: (i, 0))]
    kern = functools.partial(_out_mlp_kernel, n_h=len(h_args), n_act=tuple(n_act), n_p=n_p, split_out=split_out)
    return pl.pallas_call(
        kern, grid=(n_t, d_ff // tf), in_specs=in_specs, out_specs=out_specs, out_shape=out_shape,
        scratch_shapes=[pltpu.VMEM((tm, d), F32), pltpu.VMEM((tm, d), BF16), pltpu.VMEM((tm, d), F32)],
        compiler_params=_params(("arbitrary", "arbitrary")), name="out_mlp")(*args)


def kernel(x_prompt, x_sample, cache_k, cache_v, page_table, state_conv, state_ssm, cache_win_k, cache_win_v,
           norm_mix, norm_mlp, w_up, w_down, a_w_in, a_conv_w, a_conv_b, a_dt_bias, a_A_log, a_D, a_ssd_norm,
           a_q_norm, a_k_norm, a_lam_q1, a_lam_k1, a_lam_q2, a_lam_k2, a_subln, a_w_out,
           c_w_in, c_q_norm, c_k_norm, c_sinks, c_w_out):
    batch, seq, d = x_prompt.shape
    dec_batch, dec_seq, _ = x_sample.shape
    page = cache_k.shape[2]
    past_len = page_table.shape[1] * page
    t_p, t_s = batch * seq, dec_batch * dec_seq
    assert a_w_in.shape[0] == 1 and c_w_in.shape[0] == 1 and norm_mix.shape[0] == 2, "kernel is written for depth 2"
    assert d % (2 * LANES) == 0 and seq % SSD_CHUNK == 0 and SSD_CHUNK % dec_seq == 0
    assert dec_batch % (SSD_CHUNK // dec_seq) == 0 and cache_win_k.shape[2] == WINDOW

    tm = next(t for t in (512, 256, 128) if t_p % t == 0 and t_s % t == 0 and seq % t == 0 and t % dec_seq == 0)
    n_p, n_t = t_p // tm, (t_p + t_s) // tm

    xp = x_prompt.reshape(t_p, d)
    xs = x_sample.reshape(t_s, d)
    rope = _rope_tables(seq, past_len, dec_seq, tm)
    ii = jnp.arange(LANES)
    bd = jnp.where((ii[:, None] // HEAD) == (ii[None, :] // HEAD), 1.0 / HEAD, 0.0).astype(BF16)

    xw = d
    cw = xw + 2 * SSD_GROUPS * SSD_STATE
    n_h = xw // SSD_HEAD_DIM
    hpg = n_h // SSD_GROUPS
    qw = d
    kvw = DIFF_KV_HEADS * 2 * DIFF_HEAD_DIM
    offs = [0, xw, xw + cw, xw + cw + n_h, xw + cw + n_h + qw, xw + cw + n_h + qw + kvw]
    w_in = a_w_in[0]
    w_z, w_xbc, w_dt, w_q, w_k, w_v = (w_in[:, offs[0]:offs[1]], w_in[:, offs[1]:offs[2]], w_in[:, offs[2]:offs[3]],
                                       w_in[:, offs[3]:offs[4]], w_in[:, offs[4]:offs[5]], w_in[:, offs[5]:])
    w_dt_g = jnp.pad(w_dt.reshape(d, SSD_GROUPS, hpg), ((0, 0), (0, 0), (0, LANES - hpg))).reshape(d, SSD_GROUPS * LANES)
    bf = lambda w: w.astype(BF16)
    segs0 = [
        dict(w=bf(w_z), kind="plain", width=xw, outs=[dict(dtype=BF16)]),
        dict(w=bf(w_xbc), kind="plain", width=cw, outs=[dict(dtype=F32)]),
        dict(w=bf(w_dt_g), kind="plain", width=SSD_GROUPS * LANES, outs=[dict(dtype=F32)]),
        dict(w=bf(w_q), kind="qk", width=qw, gain=a_q_norm[0], scale=DIFF_HEAD_DIM ** -0.5, outs=[dict(dtype=BF16)]),
        dict(w=bf(w_k), kind="qk", width=kvw, gain=a_k_norm[0], scale=1.0, outs=[dict(dtype=F32), dict(dtype=BF16)]),
        dict(w=bf(w_v), kind="plain", width=kvw, outs=[dict(dtype=F32), dict(dtype=BF16)]),
    ]
    z, xbc, dt, q0, k0, k0b, v0, v0b = _norm_proj((xp, xs), norm_mix[0], segs0, rope, bd, tm, n_p, n_t)

    pad_l = lambda v: jnp.pad(v.astype(F32).reshape(SSD_GROUPS, hpg), ((0, 0), (0, LANES - hpg))).reshape(1, SSD_GROUPS * LANES)
    dtb, alog = pad_l(a_dt_bias[0]), pad_l(a_A_log[0])
    d_x = jnp.repeat(a_D[0].astype(F32), SSD_HEAD_DIM).reshape(1, xw)
    nw = a_ssd_norm[0].astype(F32).reshape(1, xw)
    gw = xw // SSD_GROUPS
    eexp = jnp.where(ii[:, None] == (jnp.arange(gw)[None, :] // SSD_HEAD_DIM), 1.0, 0.0).astype(BF16)
    conv_w, conv_b = a_conv_w[0].astype(F32), a_conv_b[0].astype(F32).reshape(1, cw)

    y_p, p_ssm = _ssd_prompt(xbc, dt, z, conv_w, conv_b, dtb, alog, d_x, nw, eexp, batch, seq)
    prev8 = jnp.pad(state_conv[0].astype(F32), ((0, 0), (8 - (SSD_CONV - 1), 0), (0, 0))).reshape(dec_batch * 8, cw)
    assert dec_seq == 8, "sample conv halo layout assumes 8-token sequences"
    y_s, s_ssm = _ssd_sample(xbc, prev8, dt, z, conv_w, conv_b, dtb, alog, d_x, nw, eexp,
                             state_ssm[0].astype(F32), t_p, dec_batch, dec_seq)

    lam_init0 = 0.8 - 0.6 * math.exp(-0.3 * 0)
    lam_vecs = [v[0].astype(F32).reshape(1, DIFF_HEAD_DIM) for v in (a_lam_q1, a_lam_k1, a_lam_q2, a_lam_k2)]
    subln = a_subln[0].astype(F32).reshape(1, LANES)
    o_p = _diff_prompt(q0, k0b, v0b, lam_vecs, subln, batch, seq, lam_init0)
    n_pool = cache_k.shape[1]
    o_s = _diff_paged(q0[t_p:].astype(F32), k0, v0, cache_k[0].reshape(n_pool, page * DIFF_KV_HEADS, LANES),
                      cache_v[0].reshape(n_pool, page * DIFF_KV_HEADS, LANES), page_table.astype(jnp.int32), lam_vecs, subln,
                      t_p, dec_batch, dec_seq, lam_init0)

    (h1,) = _out_mlp((xp, xs), [(y_p, y_s), (o_p, o_s)], bf(a_w_out[0]), norm_mlp[0], bf(w_up[0]), bf(w_down[0]),
                     tm, n_p, n_t, split_out=False)

    cqw = d
    ckw = SWA_KV_HEADS * SWA_HEAD_DIM
    wc = c_w_in[0]
    segs1 = [
        dict(w=bf(wc[:, :cqw]), kind="qk", width=cqw, gain=c_q_norm[0], scale=SWA_HEAD_DIM ** -0.5, outs=[dict(dtype=BF16)]),
        dict(w=bf(wc[:, cqw:cqw + ckw]), kind="qk", width=ckw, gain=c_k_norm[0], scale=1.0,
             outs=[dict(dtype=F32), dict(dtype=BF16, dup=True)]),
        dict(w=bf(wc[:, cqw + ckw:]), kind="dupv", width=ckw, outs=[dict(dtype=F32), dict(dtype=BF16, dup=True)]),
    ]
    q1, k1, k1d, v1, v1d = _norm_proj(h1, norm_mix[1], segs1, rope, bd, tm, n_p, n_t)
    sinks = c_sinks[0].astype(F32).reshape(1, -1)
    o1_p = _swa_prompt(q1, k1d, v1d, sinks, batch, seq)
    o1_s = _swa_sample(q1[t_p:].astype(F32), cache_win_k[0].reshape(dec_batch, WINDOW, ckw),
                       cache_win_v[0].reshape(dec_batch, WINDOW, ckw), k1, v1, sinks, t_p, dec_batch, dec_seq)
    y_prompt, y_sample = _out_mlp(h1, [(o1_p, o1_s)], bf(c_w_out[0]), norm_mlp[1], bf(w_up[1]), bf(w_down[1]),
                                  tm, n_p, n_t, split_out=True)

    n_kv = DIFF_KV_HEADS
    p_k = k0[:t_p].reshape(1, batch, seq, n_kv, kvw // n_kv)
    p_v = v0[:t_p].reshape(1, batch, seq, n_kv, kvw // n_kv)
    s_k = k0[t_p:].reshape(1, dec_batch, dec_seq, n_kv, kvw // n_kv)
    s_v = v0[t_p:].reshape(1, dec_batch, dec_seq, n_kv, kvw // n_kv)
    keep = SSD_CONV - 1
    p_conv = xbc[:t_p].reshape(batch, seq, cw)[:, seq - keep:][None]
    s_conv = jnp.concatenate([state_conv[0].astype(F32), xbc[t_p:].reshape(dec_batch, dec_seq, cw)], axis=1)[:, -keep:][None]
    wk_p = min(WINDOW, seq)
    k1p = k1[:t_p].reshape(batch, seq, SWA_KV_HEADS, SWA_HEAD_DIM)
    v1p = v1[:t_p].reshape(batch, seq, SWA_KV_HEADS, SWA_HEAD_DIM)
    p_wk, p_wv = k1p[:, seq - wk_p:][None], v1p[:, seq - wk_p:][None]
    k1s = k1[t_p:].reshape(dec_batch, dec_seq, SWA_KV_HEADS, SWA_HEAD_DIM)
    v1s = v1[t_p:].reshape(dec_batch, dec_seq, SWA_KV_HEADS, SWA_HEAD_DIM)
    wb = cache_win_k.shape[2]
    s_wk = jnp.concatenate([cache_win_k[0].astype(F32), k1s], axis=1)[:, -wb:][None]
    s_wv = jnp.concatenate([cache_win_v[0].astype(F32), v1s], axis=1)[:, -wb:][None]
    return (y_prompt.reshape(batch, seq, d), y_sample.reshape(dec_batch, dec_seq, d),
            p_k, p_v, s_k, s_v, p_conv, s_conv, p_ssm[None], s_ssm[None], p_wk, p_wv, s_wk, s_wv)
```

```python
import functools
import math

import jax
import jax.numpy as jnp
from jax import lax
from jax.experimental import pallas as pl
from jax.experimental.pallas import tpu as pltpu

F32 = jnp.float32
BF16 = jnp.bfloat16

SSD_HEAD_DIM = 64
SSD_GROUPS = 2
SSD_STATE = 128
SSD_CONV = 4
SSD_CHUNK = 128
DIFF_HEAD_DIM = 64
DIFF_KV_HEADS = 4
SWA_HEAD_DIM = 64
SWA_KV_HEADS = 4
WINDOW = 128
ROPE_THETA = 500000.0
ROPE_ROT = 16
EPS = 1e-6
LOG2E = math.log2(math.e)

LANES = 128
HEAD = 64
VMEM_LIMIT = 56 * 1024 * 1024


def _dot(a, b):
    return jnp.dot(a, b, preferred_element_type=F32)


def _dot_nt(a, b):
    return lax.dot_general(a, b, (((1,), (1,)), ((), ())), preferred_element_type=F32)


def _split2_dot(v, m):
    hi = v.astype(BF16)
    lo = (v - hi.astype(F32)).astype(BF16)
    return _dot(hi, m) + _dot(lo, m)


def _split3_dot(m, v):
    hi = v.astype(BF16)
    r1 = v - hi.astype(F32)
    mid = r1.astype(BF16)
    lo = (r1 - mid.astype(F32)).astype(BF16)
    return _dot(m, hi) + _dot(m, mid) + _dot(m, lo)


def _silu(x):
    return x * (1.0 / (1.0 + jnp.exp(-x)))


def _softplus(x):
    return jnp.maximum(x, 0.0) + jnp.log(1.0 + jnp.exp(-jnp.abs(x)))


def _lane_half(shape):
    return (lax.broadcasted_iota(jnp.int32, shape, len(shape) - 1) % LANES) // HEAD


def _params(sem, vmem=VMEM_LIMIT):
    return pltpu.CompilerParams(dimension_semantics=sem, vmem_limit_bytes=vmem)


def _const_spec(shape):
    nd = len(shape)
    return pl.BlockSpec(shape, lambda *_: (0,) * nd)


def _row_specs(src, tm, n_p):
    if isinstance(src, tuple):
        p, s = src
        wp = p.shape[1]
        return [pl.BlockSpec((tm, wp), lambda i, *_: (jnp.minimum(i, n_p - 1), 0)),
                pl.BlockSpec((tm, wp), lambda i, *_: (jnp.maximum(i - n_p, 0), 0))], [p, s]
    return [pl.BlockSpec((tm, src.shape[1]), lambda i, *_: (i, 0))], [src]


def _row_load(refs, i, n_p, dtype):
    if len(refs) == 2:
        return jnp.where(i < n_p, refs[0][...].astype(dtype), refs[1][...].astype(dtype))
    return refs[0][...].astype(dtype)


def _rope_tables(seq, past_len, dec_seq, tm):
    half = ROPE_ROT // 2
    inv_freq = ROPE_THETA ** (-jnp.arange(half, dtype=F32) * (2.0 / ROPE_ROT))
    pos = jnp.concatenate([jnp.arange(seq, dtype=jnp.int32),
                           past_len + (jnp.arange(tm, dtype=jnp.int32) % dec_seq)]).astype(F32)
    ang = pos[:, None] * inv_freq[None, :]
    cos, sin = jnp.cos(ang), jnp.sin(ang)
    d = jnp.arange(LANES) % HEAD
    cos_l = jnp.where(d < ROPE_ROT, cos[:, d % half], 1.0)
    sin_a = jnp.where(d < half, -sin[:, d % half], 0.0)
    sin_b = jnp.where((d >= half) & (d < ROPE_ROT), sin[:, d % half], 0.0)
    return cos_l.astype(F32), sin_a.astype(F32), sin_b.astype(F32)


def _head_norm_rope(x, gain, bd, cos_l, sin_a, sin_b, scale):
    ms = _dot((x * x).astype(BF16), bd)
    y = x * lax.rsqrt(ms + EPS) * gain
    half = ROPE_ROT // 2
    y = y * cos_l + pltpu.roll(y, LANES - half, 1) * sin_a + pltpu.roll(y, half, 1) * sin_b
    if scale != 1.0:
        y = y * scale
    return y


def _norm_proj_kernel(*refs, n_src, n_p, segs, n_out):
    src = refs[:n_src]
    g_ref, cos_ref, sa_ref, sb_ref, bd_ref = refs[n_src:n_src + 5]
    pos = n_src + 5
    w_refs = refs[pos:pos + len(segs)]
    pos += len(segs)
    n_gain = sum(1 for s in segs if s["kind"] == "qk")
    gain_refs = refs[pos:pos + n_gain]
    pos += n_gain
    out_refs = refs[pos:pos + n_out]

    i = pl.program_id(0)
    x = _row_load(src, i, n_p, F32)
    xn = (x * lax.rsqrt(jnp.mean(x * x, axis=-1, keepdims=True) + EPS) * g_ref[...]).astype(BF16)

    oi = 0
    gi = 0
    tm = x.shape[0]
    head_rows = []
    for s, w_ref in zip(segs, w_refs):
        width = s["width"]
        n_refs = sum(2 if spec.get("rows") else 1 for spec in s["outs"])
        outs = out_refs[oi:oi + n_refs]
        oi += n_refs
        if s["kind"] == "qk":
            gain = gain_refs[gi][...]
            gi += 1
        step = min(width, 512)
        lane_wise = s["kind"] != "plain" or any(spec.get("rows") or spec.get("dup") for spec in s["outs"])
        for c0 in range(0, width, step):
            acc = _dot(xn, w_ref[:, c0:c0 + step])
            if not lane_wise:
                for o_ref, spec in zip(outs, s["outs"]):
                    o_ref[:, c0:c0 + step] = acc.astype(spec["dtype"])
                continue
            for l0 in range(0, step, LANES):
                y = acc[:, l0:l0 + LANES]
                if s["kind"] == "qk":
                    y = _head_norm_rope(y, gain, bd_ref[...], cos_ref[...], sa_ref[...], sb_ref[...], s["scale"])
                col = c0 + l0
                ri = 0
                for spec in s["outs"]:
                    o_ref = outs[ri]
                    if spec.get("rows"):
                        head_rows.append((outs[ri], outs[ri + 1], pl.ds(col // LANES, tm, stride=spec["rows"]), y))
                        ri += 2
                        continue
                    ri += 1
                    if spec.get("dup"):
                        sw = pltpu.roll(y, HEAD, 1)
                        first = _lane_half(y.shape) == 0
                        o_ref[:, 2 * col:2 * col + LANES] = jnp.where(first, y, sw).astype(spec["dtype"])
                        o_ref[:, 2 * col + LANES:2 * col + 2 * LANES] = jnp.where(first, sw, y).astype(spec["dtype"])
                    else:
                        o_ref[:, col:col + LANES] = y.astype(spec["dtype"])

    if head_rows:
        @pl.when(i < n_p)
        def _():
            for op_ref, _, tgt, y in head_rows:
                op_ref[tgt, :] = y.astype(op_ref.dtype)

        @pl.when(i >= n_p)
        def _():
            for _, os_ref, tgt, y in head_rows:
                os_ref[tgt, :] = y.astype(os_ref.dtype)


def _norm_proj(src, gain, segs, rope, bd, tm, n_p, n_t):
    d_model = gain.shape[-1]
    row_specs, row_args = _row_specs(src, tm, n_p)
    cos_l, sin_a, sin_b = rope
    n_pb = (cos_l.shape[0] - tm) // tm
    rope_spec = pl.BlockSpec((tm, LANES), lambda i: (jnp.where(i < n_p, i % n_pb, n_pb), 0))
    in_specs = row_specs + [_const_spec((1, d_model)), rope_spec, rope_spec, rope_spec, _const_spec((LANES, LANES))]
    args = row_args + [gain.reshape(1, d_model), cos_l, sin_a, sin_b, bd]
    for s in segs:
        in_specs.append(_const_spec(s["w"].shape))
        args.append(s["w"])
    for s in segs:
        if s["kind"] == "qk":
            in_specs.append(_const_spec((1, LANES)))
            args.append(jnp.tile(s["gain"].astype(F32), LANES // HEAD).reshape(1, LANES))
    out_shapes, out_specs = [], []
    for s in segs:
        for spec in s["outs"]:
            if spec.get("rows"):
                nr = spec["rows"]
                assert s["width"] == nr * LANES
                out_shapes += [jax.ShapeDtypeStruct((n_p * tm * nr, LANES), spec["dtype"]),
                               jax.ShapeDtypeStruct(((n_t - n_p) * tm * nr, LANES), spec["dtype"])]
                out_specs += [pl.BlockSpec((tm * nr, LANES), lambda i: (jnp.minimum(i, n_p - 1), 0)),
                              pl.BlockSpec((tm * nr, LANES), lambda i: (jnp.maximum(i - n_p, 0), 0))]
                continue
            w = s["width"] * (2 if spec.get("dup") else 1)
            out_shapes.append(jax.ShapeDtypeStruct((n_t * tm, w), spec["dtype"]))
            out_specs.append(pl.BlockSpec((tm, w), lambda i: (i, 0)))
    kern = functools.partial(
        _norm_proj_kernel, n_src=len(row_args), n_p=n_p,
        segs=[{k: v for k, v in s.items() if k not in ("w", "gain")} for s in segs], n_out=len(out_shapes))
    return pl.pallas_call(
        kern, grid=(n_t,), in_specs=in_specs, out_specs=out_specs, out_shape=out_shapes,
        compiler_params=_params(("arbitrary",)), name="norm_proj")(*args)


def _conv_rolls(u, prevpad, w4, bias, l_sub):
    rows = lax.broadcasted_iota(jnp.int32, u.shape, 0) % l_sub
    n = u.shape[0]
    y = bias + w4[SSD_CONV - 1:SSD_CONV, :] * u
    for k in range(SSD_CONV - 1):
        j = SSD_CONV - 1 - k
        cur = pltpu.roll(u, j, 0)
        prv = pltpu.roll(prevpad, n + j - 8, 0)
        y = y + w4[k:k + 1, :] * jnp.where(rows >= j, cur, prv)
    return y


def _ssd_block(xs_c, bm, cm, dt_raw, z, dtb, a_log, d_x, nw, eexp, hprev_fn, hstore_fn, l_sub):
    n = xs_c.shape[0]
    n_seq = n // l_sub
    hpg = xs_c.shape[1] // SSD_HEAD_DIM
    row = lax.broadcasted_iota(jnp.int32, (n, n), 0)
    col = lax.broadcasted_iota(jnp.int32, (n, n), 1)
    same = (row // l_sub) == (col // l_sub)
    tri = same & (col <= row)

    dt = _softplus(dt_raw + dtb)
    a = dt * (-jnp.exp(a_log))
    a_cum = _split3_dot(jnp.where(tri, 1.0, 0.0).astype(BF16), a)
    a_tot = _split3_dot(jnp.where(same, 1.0, 0.0).astype(BF16), a)
    a_cum_t = a_cum.T
    e_tot_t = jnp.exp(a_tot.T)

    xdt = xs_c * _split2_dot(dt, eexp)
    bm_b = bm.astype(BF16)
    cm_b = cm.astype(BF16)
    g_cb = _dot_nt(cm_b, bm_b)

    half = _lane_half((n, LANES))
    y_parts = []
    for pr in range(hpg // 2):
        ms = []
        for hh in range(2):
            h = 2 * pr + hh
            diff = a_cum[:, h:h + 1] - a_cum_t[h:h + 1, :]
            ms.append((g_cb * jnp.exp(jnp.where(tri, diff, -jnp.inf))).astype(BF16))
        xp = xdt[:, pr * LANES:(pr + 1) * LANES]
        xcat = jnp.concatenate([jnp.where(half == 0, xp, 0.0), jnp.where(half == 1, xp, 0.0)], axis=0).astype(BF16)
        y_parts.append(_dot(jnp.concatenate(ms, axis=1), xcat))
    y = jnp.concatenate(y_parts, axis=1)

    xw_t = (xdt * _split2_dot(jnp.exp(a_tot - a_cum), eexp)).T.astype(BF16)
    seq_row = lax.broadcasted_iota(jnp.int32, (n, 1), 0) // l_sub
    y_off = jnp.zeros_like(y)
    for b in range(n_seq):
        h_b = hprev_fn(b)
        r_b = _dot_nt(cm_b, h_b.reshape(hpg * SSD_HEAD_DIM, SSD_STATE).astype(BF16))
        bm_sel = bm_b if n_seq == 1 else jnp.where(seq_row == b, bm, 0.0).astype(BF16)
        s_b = _dot(xw_t, bm_sel)
        y_off = r_b if n_seq == 1 else jnp.where(seq_row == b, r_b, y_off)
        t0 = b * l_sub
        for h in range(hpg):
            hstore_fn(b, h, e_tot_t[h:h + 1, t0:t0 + 1] * h_b[h] + s_b[h * SSD_HEAD_DIM:(h + 1) * SSD_HEAD_DIM, :])
    y = y + y_off * _split2_dot(jnp.exp(a_cum), eexp) + xs_c * d_x

    gated = y * _silu(z.astype(F32))
    return (gated * lax.rsqrt(jnp.mean(gated * gated, axis=-1, keepdims=True) + EPS) * nw).astype(BF16)


def _ssd_prompt_kernel(xs_ref, b_ref, c_ref, hxs_ref, hb_ref, hc_ref, dt_ref, z_ref,
                       wxs_ref, wb_ref, wc_ref, bxs_ref, bb_ref, bc_ref,
                       dtb_ref, alog_ref, dx_ref, nw_ref, eexp_ref, y_ref, st_ref):
    c = pl.program_id(2)

    @pl.when(c == 0)
    def _():
        st_ref[...] = jnp.zeros_like(st_ref)

    keep = jnp.where(c > 0, 1.0, 0.0)

    def conv(u_ref, h_ref, w_ref, bias_ref):
        u = u_ref[...]
        prevpad = jnp.concatenate([h_ref[...] * keep, jnp.zeros((u.shape[0] - 8, u.shape[1]), F32)], axis=0)
        return _silu(_conv_rolls(u, prevpad, w_ref[...], bias_ref[...], u.shape[0]))

    xs_c = conv(xs_ref, hxs_ref, wxs_ref, bxs_ref)
    bm = conv(b_ref, hb_ref, wb_ref, bb_ref)
    cm = conv(c_ref, hc_ref, wc_ref, bc_ref)

    def hprev(b):
        return st_ref[0]

    def hstore(b, h, val):
        st_ref[0, h] = val

    y_ref[...] = _ssd_block(xs_c, bm, cm, dt_ref[...], z_ref[...], dtb_ref[...], alog_ref[...], dx_ref[...],
                            nw_ref[...], eexp_ref[...], hprev, hstore, xs_c.shape[0])


def _ssd_sample_kernel(xs_ref, b_ref, c_ref, pxs_ref, pb_ref, pc_ref, dt_ref, z_ref,
                       wxs_ref, wb_ref, wc_ref, bxs_ref, bb_ref, bc_ref,
                       dtb_ref, alog_ref, dx_ref, nw_ref, eexp_ref, st_in_ref, y_ref, st_ref, *, l_sub):
    def conv(u_ref, p_ref, w_ref, bias_ref):
        return _silu(_conv_rolls(u_ref[...], p_ref[...], w_ref[...], bias_ref[...], l_sub))

    xs_c = conv(xs_ref, pxs_ref, wxs_ref, bxs_ref)
    bm = conv(b_ref, pb_ref, wb_ref, bb_ref)
    cm = conv(c_ref, pc_ref, wc_ref, bc_ref)

    def hprev(b):
        return st_in_ref[b]

    def hstore(b, h, val):
        st_ref[b, h] = val

    y_ref[...] = _ssd_block(xs_c, bm, cm, dt_ref[...], z_ref[...], dtb_ref[...], alog_ref[...], dx_ref[...],
                            nw_ref[...], eexp_ref[...], hprev, hstore, l_sub)


def _ssd_common_specs(xw, gw, sw, row_of, ng):
    nb = xw // LANES
    def rs(w, colf):
        return pl.BlockSpec((SSD_CHUNK, w), lambda *g: (row_of(*g), colf(g[1])))
    return dict(
        xs=rs(gw, lambda g: g), b=rs(sw, lambda g: nb + g), c=rs(sw, lambda g: nb + ng + g),
        wxs=pl.BlockSpec((SSD_CONV, gw), lambda *g: (0, g[1])),
        wb=pl.BlockSpec((SSD_CONV, sw), lambda *g: (0, nb + g[1])),
        wc=pl.BlockSpec((SSD_CONV, sw), lambda *g: (0, nb + ng + g[1])),
        bxs=pl.BlockSpec((1, gw), lambda *g: (0, g[1])),
        bb=pl.BlockSpec((1, sw), lambda *g: (0, nb + g[1])),
        bc=pl.BlockSpec((1, sw), lambda *g: (0, nb + ng + g[1])),
        lane=pl.BlockSpec((1, LANES), lambda *g: (0, g[1])),
        grp=pl.BlockSpec((1, gw), lambda *g: (0, g[1])),
        eexp=_const_spec((LANES, gw)),
    )


def _ssd_prompt(xbc, dt, z, conv_w, conv_b, dtb, alog, d_x, nw, eexp, batch, seq):
    ng = SSD_GROUPS
    xw = z.shape[1]
    gw, sw = xw // ng, SSD_STATE
    hpg = gw // SSD_HEAD_DIM
    nc = seq // SSD_CHUNK
    row_of = lambda b, g, c: b * nc + c
    sp = _ssd_common_specs(xw, gw, sw, row_of, ng)
    nb = xw // LANES
    sub = SSD_CHUNK // 8

    def halo(w, colf):
        return pl.BlockSpec((8, w), lambda b, g, c: (jnp.maximum((b * nc + c) * sub - 1, 0), colf(g)))

    in_specs = [sp["xs"], sp["b"], sp["c"],
                halo(gw, lambda g: g), halo(sw, lambda g: nb + g), halo(sw, lambda g: nb + ng + g),
                pl.BlockSpec((SSD_CHUNK, LANES), lambda b, g, c: (row_of(b, g, c), g)),
                pl.BlockSpec((SSD_CHUNK, gw), lambda b, g, c: (row_of(b, g, c), g)),
                sp["wxs"], sp["wb"], sp["wc"], sp["bxs"], sp["bb"], sp["bc"],
                sp["lane"], sp["lane"], sp["grp"], sp["grp"], sp["eexp"]]
    out_specs = [pl.BlockSpec((SSD_CHUNK, gw), lambda b, g, c: (row_of(b, g, c), g)),
                 pl.BlockSpec((1, hpg, SSD_HEAD_DIM, SSD_STATE), lambda b, g, c: (b, g, 0, 0))]
    out_shape = [jax.ShapeDtypeStruct((batch * seq, xw), BF16),
                 jax.ShapeDtypeStruct((batch, ng * hpg, SSD_HEAD_DIM, SSD_STATE), F32)]
    return pl.pallas_call(
        _ssd_prompt_kernel, grid=(batch, ng, nc), in_specs=in_specs, out_specs=out_specs, out_shape=out_shape,
        compiler_params=_params(("arbitrary", "arbitrary", "arbitrary")), name="ssd_prompt")(
            xbc, xbc, xbc, xbc, xbc, xbc, dt, z, conv_w, conv_w, conv_w, conv_b, conv_b, conv_b,
            dtb, alog, d_x, nw, eexp)


def _ssd_sample(xbc, prev8, dt, z, conv_w, conv_b, dtb, alog, d_x, nw, eexp, state, t_p, dec_batch, dec_seq):
    ng = SSD_GROUPS
    xw = z.shape[1]
    gw, sw = xw // ng, SSD_STATE
    hpg = gw // SSD_HEAD_DIM
    n_seq = SSD_CHUNK // dec_seq
    nblk = dec_batch // n_seq
    rb0 = t_p // SSD_CHUNK
    row_of = lambda i, g: rb0 + i
    sp = _ssd_common_specs(xw, gw, sw, row_of, ng)
    nb = xw // LANES

    def prev(w, colf):
        return pl.BlockSpec((SSD_CHUNK, w), lambda i, g: (i, colf(g)))

    st_spec = pl.BlockSpec((n_seq, hpg, SSD_HEAD_DIM, SSD_STATE), lambda i, g: (i, g, 0, 0))
    in_specs = [sp["xs"], sp["b"], sp["c"],
                prev(gw, lambda g: g), prev(sw, lambda g: nb + g), prev(sw, lambda g: nb + ng + g),
                pl.BlockSpec((SSD_CHUNK, LANES), lambda i, g: (rb0 + i, g)),
                pl.BlockSpec((SSD_CHUNK, gw), lambda i, g: (rb0 + i, g)),
                sp["wxs"], sp["wb"], sp["wc"], sp["bxs"], sp["bb"], sp["bc"],
                sp["lane"], sp["lane"], sp["grp"], sp["grp"], sp["eexp"], st_spec]
    out_specs = [pl.BlockSpec((SSD_CHUNK, gw), lambda i, g: (i, g)), st_spec]
    out_shape = [jax.ShapeDtypeStruct((dec_batch * dec_seq, xw), BF16),
                 jax.ShapeDtypeStruct(state.shape, F32)]
    return pl.pallas_call(
        functools.partial(_ssd_sample_kernel, l_sub=dec_seq), grid=(nblk, ng), in_specs=in_specs,
        out_specs=out_specs, out_shape=out_shape,
        compiler_params=_params(("arbitrary", "arbitrary")), name="ssd_sample")(
            xbc, xbc, xbc, prev8, prev8, prev8, dt, z, conv_w, conv_w, conv_w, conv_b, conv_b, conv_b,
            dtb, alog, d_x, nw, eexp, state)


def _lambda(lq1, lk1, lq2, lk2, lam_init):
    s1 = jnp.sum(lq1[...] * lk1[...], axis=-1, keepdims=True)
    s2 = jnp.sum(lq2[...] * lk2[...], axis=-1, keepdims=True)
    return jnp.exp(s1) - jnp.exp(s2) + lam_init


def _subln(att, w, lam_init):
    return att * lax.rsqrt(jnp.mean(att * att, axis=-1, keepdims=True) + EPS) * w * (1.0 - lam_init)


def _lane_tile(x, n):
    return x if n == 1 else jnp.concatenate([x] * n, axis=1)


def _diff_prompt_kernel(q_ref, k_ref, v_ref, lq1, lk1, lq2, lk2, sub_ref, o_ref,
                        lhs_ref, vext_ref, s_ref, m_ref, acc_ref, *, tq, tk, lam_init):
    qb = pl.program_id(2)
    n_g = q_ref.shape[1] // LANES
    rows = 2 * n_g * tq
    nl = tk // LANES

    @pl.when(qb == 0)
    def _():
        vext_ref[:, :LANES] = v_ref[...]
        vext_ref[:, LANES:] = jnp.ones((v_ref.shape[0], LANES), vext_ref.dtype)

    half = _lane_half((tq, LANES))
    for j in range(2):
        for g in range(n_g):
            qg = q_ref[:, g * LANES:(g + 1) * LANES]
            r0 = (j * n_g + g) * tq
            lhs_ref[r0:r0 + tq, :] = jnp.where(half == j, qg, jnp.zeros_like(qg))
    m_ref[...] = jnp.full(m_ref.shape, -jnp.inf, F32)
    acc_ref[...] = jnp.zeros(acc_ref.shape, F32)
    n_t = (qb * tq) // tk + 1

    def scores(t, masked):
        k = k_ref[pl.ds(pl.multiple_of(t * tk, tk), tk), :]
        s = _dot_nt(lhs_ref[...], k)
        if masked:
            qpos = qb * tq + lax.broadcasted_iota(jnp.int32, (rows, tk), 0) % tq
            kpos = t * tk + lax.broadcasted_iota(jnp.int32, (rows, tk), 1)
            s = jnp.where(kpos <= qpos, s, -jnp.inf)
        s_ref[t] = s
        mt = s[:, :LANES]
        for c in range(1, nl):
            mt = jnp.maximum(mt, s[:, c * LANES:(c + 1) * LANES])
        m_ref[...] = jnp.maximum(m_ref[...], mt)

    lax.fori_loop(0, n_t - 1, lambda t, c: (scores(t, False), c)[1], 0)
    scores(n_t - 1, True)
    m_ref[...] = jnp.broadcast_to(jnp.max(m_ref[...], axis=-1, keepdims=True), m_ref.shape)

    def attend(t, c):
        p = jnp.exp2(s_ref[t] - _lane_tile(m_ref[...], nl)).astype(BF16)
        acc_ref[...] += _dot(p, vext_ref[pl.ds(pl.multiple_of(t * tk, tk), tk), :])
        return c

    lax.fori_loop(0, n_t, attend, 0)

    lam = _lambda(lq1, lk1, lq2, lk2, lam_init)
    acc = acc_ref[...]
    o = acc[:, :LANES] / acc[:, LANES:]
    att = o[:rows // 2] - lam * o[rows // 2:]
    res = _subln(att, sub_ref[...], lam_init)
    for g in range(n_g):
        o_ref[:, g * LANES:(g + 1) * LANES] = res[g * tq:(g + 1) * tq].astype(o_ref.dtype)


def _diff_prompt(q, kb, vb, lam_vecs, subln, batch, seq, lam_init):
    t_p = batch * seq
    n_kvh = DIFF_KV_HEADS
    qw = q.shape[1] // n_kvh
    kw = kb.shape[1] // n_kvh
    tq = min(256, seq)
    tk = min(512, seq)
    assert seq % tk == 0 and tk % tq == 0 and kw == LANES
    nq = seq // tq
    rows = 2 * (qw // LANES) * tq
    vec = _const_spec((1, DIFF_HEAD_DIM))
    in_specs = [pl.BlockSpec((tq, qw), lambda b, h, i: (b * nq + i, h)),
                pl.BlockSpec((seq, kw), lambda b, h, i: (b, h)),
                pl.BlockSpec((seq, kw), lambda b, h, i: (b, h)),
                vec, vec, vec, vec, _const_spec((1, LANES))]
    return pl.pallas_call(
        functools.partial(_diff_prompt_kernel, tq=tq, tk=tk, lam_init=lam_init),
        grid=(batch, n_kvh, nq), in_specs=in_specs,
        out_specs=pl.BlockSpec((tq, qw), lambda b, h, i: (b * nq + i, h)),
        out_shape=jax.ShapeDtypeStruct((t_p, q.shape[1]), BF16),
        scratch_shapes=[pltpu.VMEM((rows, LANES), BF16), pltpu.VMEM((seq, 2 * LANES), BF16),
                        pltpu.VMEM((seq // tk, rows, tk), F32), pltpu.VMEM((rows, LANES), F32),
                        pltpu.VMEM((rows, 2 * LANES), F32)],
        compiler_params=_params(("arbitrary", "arbitrary", "arbitrary")), name="diff_attn_prompt")(
            q, kb, vb, *lam_vecs, subln)


def _diff_paged_kernel(pt_ref, q_ref, kn_ref, vn_ref, ck_hbm, cv_hbm, lq1, lk1, lq2, lk2, sub_ref, o_ref,
                       kbuf, vbuf, sem, qst_ref, bias_ref, m_ref, acc_ref, *, n_pg, pg_stage, dec_seq, lam_init):
    p = pl.program_id(1)
    n_steps = pl.num_programs(1)
    step = pl.program_id(0) * n_steps + p
    last_step = pl.num_programs(0) * n_steps - 1
    slot = step % 2
    n_kvh = DIFF_KV_HEADS
    n_g = q_ref.shape[1] // (n_kvh * LANES)
    rph = n_g * 2 * dec_seq
    rows = n_kvh * rph
    prow = kbuf.shape[2]

    def page_copies(s, sl, lookup):
        bb, pp = s // n_steps, s % n_steps
        out = []
        for i in range(n_pg):
            pg = pt_ref[bb, pp * n_pg + i] if lookup else 0
            out.append(pltpu.make_async_copy(ck_hbm.at[pg], kbuf.at[sl, i], sem.at[sl]))
            out.append(pltpu.make_async_copy(cv_hbm.at[pg], vbuf.at[sl, i], sem.at[sl]))
        return out

    @pl.when(step == 0)
    def _():
        for c in page_copies(step, slot, True):
            c.start()

    @pl.when(step < last_step)
    def _():
        for c in page_copies(step + 1, 1 - slot, True):
            c.start()

    for c in page_copies(step, slot, False):
        c.wait()
    k_pages = [kbuf.at[slot, i] for i in range(n_pg)]
    v_pages = [vbuf.at[slot, i] for i in range(n_pg)]

    @pl.when(p == 0)
    def _():
        half = _lane_half((dec_seq, LANES))
        parts = []
        for h in range(n_kvh):
            for g in range(n_g):
                qg = q_ref[:, (h * n_g + g) * LANES:(h * n_g + g + 1) * LANES]
                for j in range(2):
                    parts.append(jnp.where(half == j, qg, 0.0))
        qst_ref[...] = jnp.concatenate(parts, axis=0).astype(BF16)
        head_r = lax.broadcasted_iota(jnp.int32, (rows, prow), 0) // rph
        head_c = lax.broadcasted_iota(jnp.int32, (rows, prow), 1) % n_kvh
        bias_ref[...] = jnp.where(head_r == head_c, 0.0, -jnp.inf)
        m_ref[...] = jnp.full(m_ref.shape, -jnp.inf, F32)
        acc_ref[...] = jnp.zeros(acc_ref.shape, F32)

    def update(ss, bias, vals, lane_ok):
        mt = None
        for s in ss:
            for c in range(s.shape[1] // LANES):
                blk = s[:, c * LANES:(c + 1) * LANES]
                mt = blk if mt is None else jnp.maximum(mt, blk)
        m_old = m_ref[...]
        m_new = jnp.maximum(m_old, jnp.max(jnp.where(lane_ok, mt, -jnp.inf), axis=-1, keepdims=True))
        corr = jnp.exp2(m_old - m_new)
        shift = bias - _lane_tile(m_new, bias.shape[1] // LANES)
        pv = None
        for s, v in zip(ss, vals):
            pr = jnp.exp2(s + _lane_tile(shift, s.shape[1] // bias.shape[1])).astype(BF16)
            vext = jnp.concatenate([v.astype(BF16), jnp.ones(v.shape, BF16)], axis=1)
            t = _dot(pr, vext)
            pv = t if pv is None else pv + t
        acc_ref[...] = acc_ref[...] * _lane_tile(corr, 2) + pv
        m_ref[...] = m_new

    qst = qst_ref[...]
    own_lane = (lax.broadcasted_iota(jnp.int32, (rows, LANES), 1) % n_kvh
                == lax.broadcasted_iota(jnp.int32, (rows, LANES), 0) // rph)
    for i0 in range(0, n_pg, pg_stage):
        grp = range(i0, min(i0 + pg_stage, n_pg))
        update([_dot_nt(qst, k_pages[i][...].astype(BF16)) for i in grp], bias_ref[...],
               [v_pages[i][...] for i in grp], own_lane)

    @pl.when(p == pl.num_programs(1) - 1)
    def _():
        lam = _lambda(lq1, lk1, lq2, lk2, lam_init)
        pad = jnp.zeros((LANES - n_kvh * dec_seq, LANES), F32)
        kn = jnp.concatenate([kn_ref[...], pad], axis=0)
        vn = jnp.concatenate([vn_ref[...], pad], axis=0)
        ri = lax.broadcasted_iota(jnp.int32, (rows, LANES), 0)
        ci = lax.broadcasted_iota(jnp.int32, (rows, LANES), 1)
        ok = (ci % n_kvh == ri // rph) & (ci // n_kvh <= ri % dec_seq)
        update([_dot_nt(qst, kn.astype(BF16))], jnp.where(ok, 0.0, -jnp.inf), [vn], ok)
        acc = acc_ref[...]
        o = acc[:, :LANES] / acc[:, LANES:]
        for h in range(n_kvh):
            for g in range(n_g):
                r0 = h * rph + g * 2 * dec_seq
                att = o[r0:r0 + dec_seq] - lam * o[r0 + dec_seq:r0 + 2 * dec_seq]
                c0 = (h * n_g + g) * LANES
                o_ref[:, c0:c0 + LANES] = _subln(att, sub_ref[...], lam_init).astype(o_ref.dtype)


def _diff_paged(q_s, k_new, v_new, cache_k, cache_v, page_table, lam_vecs, subln, dec_batch, dec_seq, lam_init):
    n_pages = page_table.shape[1]
    n_pg = math.gcd(n_pages, 8)
    n_kvh = DIFF_KV_HEADS
    page_rows = cache_k.shape[1]
    n_g = q_s.shape[1] // (n_kvh * LANES)
    rows = n_kvh * n_g * 2 * dec_seq
    nr = n_kvh * dec_seq
    assert nr <= LANES and k_new.shape == (dec_batch * nr, LANES)
    vec = _const_spec((1, DIFF_HEAD_DIM))

    hbm = pl.BlockSpec(memory_space=pl.ANY)
    in_specs = [pl.BlockSpec((dec_seq, q_s.shape[1]), lambda b, p, pt: (b, 0)),
                pl.BlockSpec((nr, LANES), lambda b, p, pt: (b, 0)),
                pl.BlockSpec((nr, LANES), lambda b, p, pt: (b, 0)),
                hbm, hbm, vec, vec, vec, vec, _const_spec((1, LANES))]
    grid_spec = pltpu.PrefetchScalarGridSpec(
        num_scalar_prefetch=1, grid=(dec_batch, n_pages // n_pg), in_specs=in_specs,
        out_specs=pl.BlockSpec((dec_seq, q_s.shape[1]), lambda b, p, pt: (b, 0)),
        scratch_shapes=[pltpu.VMEM((2, n_pg, page_rows, LANES), cache_k.dtype),
                        pltpu.VMEM((2, n_pg, page_rows, LANES), cache_v.dtype),
                        pltpu.SemaphoreType.DMA((2,)),
                        pltpu.VMEM((rows, LANES), BF16), pltpu.VMEM((rows, page_rows), F32),
                        pltpu.VMEM((rows, LANES), F32), pltpu.VMEM((rows, 2 * LANES), F32)])
    return pl.pallas_call(
        functools.partial(_diff_paged_kernel, n_pg=n_pg, pg_stage=min(1, n_pg), dec_seq=dec_seq, lam_init=lam_init),
        grid_spec=grid_spec, out_shape=jax.ShapeDtypeStruct(q_s.shape, F32),
        compiler_params=_params(("arbitrary", "arbitrary")), name="diff_attn_paged")(
            page_table, q_s, k_new, v_new, cache_k, cache_v, *lam_vecs, subln)


def _sink_attend(lhs, blocks, sink):
    ss = [_dot_nt(lhs, k2) + bias for k2, _, bias in blocks]
    mt = None
    for s in ss:
        for c in range(s.shape[1] // LANES):
            blk = s[:, c * LANES:(c + 1) * LANES]
            mt = blk if mt is None else jnp.maximum(mt, blk)
    m = jnp.maximum(jnp.max(mt, axis=-1, keepdims=True), sink)
    acc = None
    for s, (_, v2, _) in zip(ss, blocks):
        p = jnp.exp2(s - _lane_tile(m, s.shape[1] // LANES)).astype(BF16)
        t = _dot(p, jnp.concatenate([v2, jnp.ones(v2.shape, BF16)], axis=1))
        acc = t if acc is None else acc + t
    return acc[:, :LANES] / (acc[:, LANES:] + jnp.exp2(sink - m))


def _swa_prompt_kernel(q_ref, kc_ref, kp_ref, vc_ref, vp_ref, sink_ref, o_ref):
    n = pl.program_id(1)
    w = q_ref.shape[0]
    n_kvh = SWA_KV_HEADS
    n_g = q_ref.shape[1] // (n_kvh * HEAD)
    half = _lane_half((w, LANES))
    qi = lax.broadcasted_iota(jnp.int32, (w, 2 * w), 0)
    ci = lax.broadcasted_iota(jnp.int32, (w, 2 * w), 1)
    ok = (ci > qi) & (ci <= qi + w) & ((ci >= w) | (n > 0))
    bias1 = jnp.where(ok, 0.0, -jnp.inf)
    bias = jnp.concatenate([bias1] * n_g, axis=0)
    sink_all = sink_ref[...] * LOG2E
    for h in range(n_kvh):
        cols = slice(h * LANES, (h + 1) * LANES)
        k2 = jnp.concatenate([kp_ref[:, cols], kc_ref[:, cols]], axis=0)
        v2 = jnp.concatenate([vp_ref[:, cols], vc_ref[:, cols]], axis=0)
        parts, sinks = [], []
        for g in range(n_g):
            hd = h * n_g + g
            qg = q_ref[:, (hd // 2) * LANES:(hd // 2 + 1) * LANES]
            parts.append(jnp.where(half == hd % 2, qg, jnp.zeros_like(qg)))
            sinks.append(jnp.broadcast_to(sink_all[0:1, hd:hd + 1], (w, LANES)))
        o2 = _sink_attend(jnp.concatenate(parts, axis=0), [(k2, v2, bias)], jnp.concatenate(sinks, axis=0))
        for a in range(n_g // 2):
            hd = h * n_g + 2 * a
            pair = jnp.where(half == 0, o2[2 * a * w:(2 * a + 1) * w], o2[(2 * a + 1) * w:(2 * a + 2) * w])
            o_ref[:, (hd // 2) * LANES:(hd // 2 + 1) * LANES] = pair.astype(o_ref.dtype)


def _swa_prompt(q, k2, v2, sinks, batch, seq):
    w = WINDOW
    nb = seq // w
    kw = k2.shape[1]
    cur = pl.BlockSpec((w, kw), lambda b, n: (b * nb + n, 0))
    prv = pl.BlockSpec((w, kw), lambda b, n: (jnp.maximum(b * nb + n - 1, 0), 0))
    in_specs = [pl.BlockSpec((w, q.shape[1]), lambda b, n: (b * nb + n, 0)), cur, prv, cur, prv,
                _const_spec(sinks.shape)]
    return pl.pallas_call(
        _swa_prompt_kernel, grid=(batch, nb), in_specs=in_specs,
        out_specs=pl.BlockSpec((w, q.shape[1]), lambda b, n: (b * nb + n, 0)),
        out_shape=jax.ShapeDtypeStruct((batch * seq, q.shape[1]), BF16),
        compiler_params=_params(("arbitrary", "arbitrary")), name="swa_prompt")(q, k2, k2, v2, v2, sinks)


def _swa_sample_kernel(q_ref, wk_ref, wv_ref, kn_ref, vn_ref, sink_ref, o_ref, *, dec_seq):
    nbb, wb, kw = wk_ref.shape
    n_kvh = SWA_KV_HEADS
    n_h = q_ref.shape[1] // HEAD
    n_g = n_h // n_kvh
    rows = n_h * dec_seq
    nv = kw // LANES
    half8 = _lane_half((dec_seq, LANES))
    qi = lax.broadcasted_iota(jnp.int32, (rows, wb), 0) % dec_seq
    ci = lax.broadcasted_iota(jnp.int32, (rows, wb), 1)
    bias_c = jnp.where(ci > qi, 0.0, -jnp.inf)
    bias_n = jnp.where(ci <= qi, 0.0, -jnp.inf)
    pad = jnp.zeros((wb - dec_seq, kw), F32)
    zero8 = jnp.zeros((dec_seq, LANES), F32)
    sink_all = sink_ref[...] * LOG2E
    sink = jnp.concatenate([jnp.broadcast_to(sink_all[0:1, hd:hd + 1], (dec_seq, LANES)) for hd in range(n_h)], axis=0)

    for bb in range(nbb):
        r0 = bb * dec_seq
        q8 = q_ref[r0:r0 + dec_seq, :]
        pieces = []
        for hd in range(n_h):
            kvh = hd // n_g
            x = jnp.where(half8 == hd % 2, q8[:, (hd // 2) * LANES:(hd // 2 + 1) * LANES], 0.0)
            if hd % 2 != kvh % 2:
                x = pltpu.roll(x, HEAD, 1)
            pieces.append(jnp.concatenate([x if c == kvh // 2 else zero8 for c in range(nv)], axis=1))
        qbd = jnp.concatenate(pieces, axis=0).astype(BF16)
        kn = jnp.concatenate([kn_ref[r0:r0 + dec_seq, :], pad], axis=0).astype(BF16)
        vn = jnp.concatenate([vn_ref[r0:r0 + dec_seq, :], pad], axis=0).astype(BF16)
        s_c = _dot_nt(qbd, wk_ref[bb].astype(BF16)) + bias_c
        s_n = _dot_nt(qbd, kn) + bias_n
        m = jnp.maximum(jnp.max(jnp.maximum(s_c, s_n), axis=-1, keepdims=True), sink)
        p_c, p_n = jnp.exp2(s_c - m), jnp.exp2(s_n - m)
        den = jnp.sum(p_c + p_n, axis=-1, keepdims=True) + jnp.exp2(sink - m)
        o_all = (_dot(p_c.astype(BF16), wv_ref[bb].astype(BF16)) + _dot(p_n.astype(BF16), vn)) / _lane_tile(den, nv)
        for pr in range(n_h // 2):
            blks = []
            for e in range(2):
                hd = 2 * pr + e
                kvh = hd // n_g
                blk = o_all[hd * dec_seq:(hd + 1) * dec_seq, (kvh // 2) * LANES:(kvh // 2 + 1) * LANES]
                blks.append(pltpu.roll(blk, HEAD, 1) if kvh % 2 != e else blk)
            o_ref[r0:r0 + dec_seq, pr * LANES:(pr + 1) * LANES] = jnp.where(half8 == 0, blks[0], blks[1])


def _swa_sample(q_s, wk, wv, k_all, v_all, sinks, t_p, dec_batch, dec_seq):
    nbb = math.gcd(dec_batch, 8)
    rows = nbb * dec_seq
    rb0 = t_p // rows
    kw = k_all.shape[1]
    wb = wk.shape[1]
    in_specs = [pl.BlockSpec((rows, q_s.shape[1]), lambda i: (i, 0)),
                pl.BlockSpec((nbb, wb, kw), lambda i: (i, 0, 0)),
                pl.BlockSpec((nbb, wb, kw), lambda i: (i, 0, 0)),
                pl.BlockSpec((rows, kw), lambda i: (rb0 + i, 0)),
                pl.BlockSpec((rows, kw), lambda i: (rb0 + i, 0)),
                _const_spec(sinks.shape)]
    return pl.pallas_call(
        functools.partial(_swa_sample_kernel, dec_seq=dec_seq), grid=(dec_batch // nbb,), in_specs=in_specs,
        out_specs=pl.BlockSpec((rows, q_s.shape[1]), lambda i: (i, 0)),
        out_shape=jax.ShapeDtypeStruct(q_s.shape, F32),
        compiler_params=_params(("arbitrary",)), name="swa_sample")(q_s, wk, wv, k_all, v_all, sinks)


def _out_mlp_kernel(*refs, n_h, n_act, n_p, split_out):
    h_refs = refs[:n_h]
    pos = n_h
    act_refs = []
    for na in n_act:
        act_refs.append(refs[pos:pos + na])
        pos += na
    wo_ref, g_ref, wu_ref, wd_ref = refs[pos:pos + 4]
    pos += 4
    n_o = 2 if split_out else 1
    out_refs = refs[pos:pos + n_o]
    h1_ref, hn_ref, acc_ref = refs[pos + n_o:]
    i = pl.program_id(0)
    j = pl.program_id(1)
    d = h1_ref.shape[1]

    @pl.when(j == 0)
    def _():
        mix = None
        for s, ar in enumerate(act_refs):
            t = _dot(_row_load(ar, i, n_p, BF16), wo_ref[s * d:(s + 1) * d, :])
            mix = t if mix is None else mix + t
        h1 = _row_load(h_refs, i, n_p, F32) + mix
        h1_ref[...] = h1
        hn_ref[...] = (h1 * lax.rsqrt(jnp.mean(h1 * h1, axis=-1, keepdims=True) + EPS) * g_ref[...]).astype(BF16)
        acc_ref[...] = jnp.zeros_like(acc_ref)

    u = jnp.maximum(_dot(hn_ref[...], wu_ref[...]), 0.0)
    acc_ref[...] += _dot((u * u).astype(BF16), wd_ref[...])

    @pl.when(j == pl.num_programs(1) - 1)
    def _():
        res = h1_ref[...] + acc_ref[...]
        if split_out:
            @pl.when(i < n_p)
            def _():
                out_refs[0][...] = res

            @pl.when(i >= n_p)
            def _():
                out_refs[1][...] = res
        else:
            out_refs[0][...] = res


def _out_mlp(h_src, act_srcs, w_out, gain, w_up, w_down, tm, n_p, n_t, split_out):
    d = gain.shape[-1]
    d_ff = w_up.shape[1]
    tf = min(1024, d_ff)
    h_specs, h_args = _row_specs(h_src, tm, n_p)
    in_specs, args, n_act = list(h_specs), list(h_args), []
    for a in act_srcs:
        sp, ar = _row_specs(a, tm, n_p)
        in_specs += sp
        args += ar
        n_act.append(len(ar))
    in_specs += [_const_spec(w_out.shape), _const_spec((1, d)),
                 pl.BlockSpec((d, tf), lambda i, j: (0, j)), pl.BlockSpec((tf, d), lambda i, j: (j, 0))]
    args += [w_out, gain.reshape(1, d), w_up, w_down]
    if split_out:
        n_s = n_t - n_p
        out_shape = [jax.ShapeDtypeStruct((n_p * tm, d), F32), jax.ShapeDtypeStruct((n_s * tm, d), F32)]
        out_specs = [pl.BlockSpec((tm, d), lambda i, j: (jnp.minimum(i, n_p - 1), 0)),
                     pl.BlockSpec((tm, d), lambda i, j: (jnp.maximum(i - n_p, 0), 0))]
    else:
        out_shape = [jax.ShapeDtypeStruct((n_t * tm, d), F32)]
        out_specs = [pl.BlockSpec((tm, d), lambda i, j: (i, 0))]
    kern = functools.partial(_out_mlp_kernel, n_h=len(h_args), n_act=tuple(n_act), n_p=n_p, split_out=split_out)
    return pl.pallas_call(
        kern, grid=(n_t, d_ff // tf), in_specs=in_specs, out_specs=out_specs, out_shape=out_shape,
        scratch_shapes=[pltpu.VMEM((tm, d), F32), pltpu.VMEM((tm, d), BF16), pltpu.VMEM((tm, d), F32)],
        compiler_params=_params(("arbitrary", "arbitrary")), name="out_mlp")(*args)


def kernel(x_prompt, x_sample, cache_k, cache_v, page_table, state_conv, state_ssm, cache_win_k, cache_win_v,
           norm_mix, norm_mlp, w_up, w_down, a_w_in, a_conv_w, a_conv_b, a_dt_bias, a_A_log, a_D, a_ssd_norm,
           a_q_norm, a_k_norm, a_lam_q1, a_lam_k1, a_lam_q2, a_lam_k2, a_subln, a_w_out,
           c_w_in, c_q_norm, c_k_norm, c_sinks, c_w_out):
    batch, seq, d = x_prompt.shape
    dec_batch, dec_seq, _ = x_sample.shape
    page = cache_k.shape[2]
    past_len = page_table.shape[1] * page
    t_p, t_s = batch * seq, dec_batch * dec_seq
    assert a_w_in.shape[0] == 1 and c_w_in.shape[0] == 1 and norm_mix.shape[0] == 2, "kernel is written for depth 2"
    assert d % (2 * LANES) == 0 and seq % SSD_CHUNK == 0 and SSD_CHUNK % dec_seq == 0
    assert dec_batch % (SSD_CHUNK // dec_seq) == 0 and cache_win_k.shape[2] == WINDOW

    tm = next(t for t in (512, 256, 128) if t_p % t == 0 and t_s % t == 0 and seq % t == 0 and t % dec_seq == 0)
    n_p, n_t = t_p // tm, (t_p + t_s) // tm

    xp = x_prompt.reshape(t_p, d)
    xs = x_sample.reshape(t_s, d)
    rope = _rope_tables(seq, past_len, dec_seq, tm)
    ii = jnp.arange(LANES)
    bd = jnp.where((ii[:, None] // HEAD) == (ii[None, :] // HEAD), 1.0 / HEAD, 0.0).astype(BF16)

    xw = d
    cw = xw + 2 * SSD_GROUPS * SSD_STATE
    n_h = xw // SSD_HEAD_DIM
    hpg = n_h // SSD_GROUPS
    qw = d
    kvw = DIFF_KV_HEADS * 2 * DIFF_HEAD_DIM
    offs = [0, xw, xw + cw, xw + cw + n_h, xw + cw + n_h + qw, xw + cw + n_h + qw + kvw]
    w_in = a_w_in[0]
    w_z, w_xbc, w_dt, w_q, w_k, w_v = (w_in[:, offs[0]:offs[1]], w_in[:, offs[1]:offs[2]], w_in[:, offs[2]:offs[3]],
                                       w_in[:, offs[3]:offs[4]], w_in[:, offs[4]:offs[5]], w_in[:, offs[5]:])
    w_dt_g = jnp.pad(w_dt.reshape(d, SSD_GROUPS, hpg), ((0, 0), (0, 0), (0, LANES - hpg))).reshape(d, SSD_GROUPS * LANES)
    bf = lambda w: w.astype(BF16)
    segs0 = [
        dict(w=bf(w_z), kind="plain", width=xw, outs=[dict(dtype=BF16)]),
        dict(w=bf(w_xbc), kind="plain", width=cw, outs=[dict(dtype=F32)]),
        dict(w=bf(w_dt_g), kind="plain", width=SSD_GROUPS * LANES, outs=[dict(dtype=F32)]),
        dict(w=bf(w_q), kind="qk", width=qw, gain=a_q_norm[0], scale=DIFF_HEAD_DIM ** -0.5 * LOG2E, outs=[dict(dtype=BF16)]),
        dict(w=bf(w_k), kind="qk", width=kvw, gain=a_k_norm[0], scale=1.0,
             outs=[dict(dtype=F32, rows=DIFF_KV_HEADS), dict(dtype=BF16)]),
        dict(w=bf(w_v), kind="plain", width=kvw, outs=[dict(dtype=F32, rows=DIFF_KV_HEADS), dict(dtype=BF16)]),
    ]
    z, xbc, dt, q0, k_p, k_s, k0b, v_p, v_s, v0b = _norm_proj((xp, xs), norm_mix[0], segs0, rope, bd, tm, n_p, n_t)

    pad_l = lambda v: jnp.pad(v.astype(F32).reshape(SSD_GROUPS, hpg), ((0, 0), (0, LANES - hpg))).reshape(1, SSD_GROUPS * LANES)
    dtb, alog = pad_l(a_dt_bias[0]), pad_l(a_A_log[0])
    d_x = jnp.repeat(a_D[0].astype(F32), SSD_HEAD_DIM).reshape(1, xw)
    nw = a_ssd_norm[0].astype(F32).reshape(1, xw)
    gw = xw // SSD_GROUPS
    eexp = jnp.where(ii[:, None] == (jnp.arange(gw)[None, :] // SSD_HEAD_DIM), 1.0, 0.0).astype(BF16)
    conv_w, conv_b = a_conv_w[0].astype(F32), a_conv_b[0].astype(F32).reshape(1, cw)

    y_p, p_ssm = _ssd_prompt(xbc, dt, z, conv_w, conv_b, dtb, alog, d_x, nw, eexp, batch, seq)
    prev8 = jnp.pad(state_conv[0].astype(F32), ((0, 0), (8 - (SSD_CONV - 1), 0), (0, 0))).reshape(dec_batch * 8, cw)
    assert dec_seq == 8, "sample conv halo layout assumes 8-token sequences"
    y_s, s_ssm = _ssd_sample(xbc, prev8, dt, z, conv_w, conv_b, dtb, alog, d_x, nw, eexp,
                             state_ssm[0].astype(F32), t_p, dec_batch, dec_seq)

    lam_init0 = 0.8 - 0.6 * math.exp(-0.3 * 0)
    lam_vecs = [v[0].astype(F32).reshape(1, DIFF_HEAD_DIM) for v in (a_lam_q1, a_lam_k1, a_lam_q2, a_lam_k2)]
    subln = a_subln[0].astype(F32).reshape(1, LANES)
    o_p = _diff_prompt(q0, k0b, v0b, lam_vecs, subln, batch, seq, lam_init0)
    n_pool = cache_k.shape[1]
    o_s = _diff_paged(q0[t_p:].astype(F32), k_s, v_s, cache_k[0].reshape(n_pool, page * DIFF_KV_HEADS, LANES),
                      cache_v[0].reshape(n_pool, page * DIFF_KV_HEADS, LANES), page_table.astype(jnp.int32), lam_vecs, subln,
                      dec_batch, dec_seq, lam_init0)

    (h1,) = _out_mlp((xp, xs), [(y_p, y_s), (o_p, o_s)], bf(a_w_out[0]), norm_mlp[0], bf(w_up[0]), bf(w_down[0]),
                     tm, n_p, n_t, split_out=False)

    cqw = d
    ckw = SWA_KV_HEADS * SWA_HEAD_DIM
    wc = c_w_in[0]
    segs1 = [
        dict(w=bf(wc[:, :cqw]), kind="qk", width=cqw, gain=c_q_norm[0], scale=SWA_HEAD_DIM ** -0.5 * LOG2E, outs=[dict(dtype=BF16)]),
        dict(w=bf(wc[:, cqw:cqw + ckw]), kind="qk", width=ckw, gain=c_k_norm[0], scale=1.0,
             outs=[dict(dtype=F32), dict(dtype=BF16, dup=True)]),
        dict(w=bf(wc[:, cqw + ckw:]), kind="dupv", width=ckw, outs=[dict(dtype=F32), dict(dtype=BF16, dup=True)]),
    ]
    q1, k1, k1d, v1, v1d = _norm_proj(h1, norm_mix[1], segs1, rope, bd, tm, n_p, n_t)
    sinks = c_sinks[0].astype(F32).reshape(1, -1)
    o1_p = _swa_prompt(q1, k1d, v1d, sinks, batch, seq)
    o1_s = _swa_sample(q1[t_p:].astype(F32), cache_win_k[0].reshape(dec_batch, WINDOW, ckw),
                       cache_win_v[0].reshape(dec_batch, WINDOW, ckw), k1, v1, sinks, t_p, dec_batch, dec_seq)
    y_prompt, y_sample = _out_mlp(h1, [(o1_p, o1_s)], bf(c_w_out[0]), norm_mlp[1], bf(w_up[1]), bf(w_down[1]),
                                  tm, n_p, n_t, split_out=True)

    n_kv = DIFF_KV_HEADS
    p_k = k_p.reshape(1, batch, seq, n_kv, kvw // n_kv)
    p_v = v_p.reshape(1, batch, seq, n_kv, kvw // n_kv)
    s_k = k_s.reshape(1, dec_batch, dec_seq, n_kv, kvw // n_kv)
    s_v = v_s.reshape(1, dec_batch, dec_seq, n_kv, kvw // n_kv)
    keep = SSD_CONV - 1
    tail = lambda a, n: jnp.stack([a[(b + 1) * seq - n:(b + 1) * seq] for b in range(batch)])
    p_conv = tail(xbc, keep)[None]
    s_conv = jnp.concatenate([state_conv[0].astype(F32), xbc[t_p:].reshape(dec_batch, dec_seq, cw)], axis=1)[:, -keep:][None]
    wk_p = min(WINDOW, seq)
    p_wk = tail(k1, wk_p).reshape(1, batch, wk_p, SWA_KV_HEADS, SWA_HEAD_DIM)
    p_wv = tail(v1, wk_p).reshape(1, batch, wk_p, SWA_KV_HEADS, SWA_HEAD_DIM)
    k1s = k1[t_p:].reshape(dec_batch, dec_seq, SWA_KV_HEADS, SWA_HEAD_DIM)
    v1s = v1[t_p:].reshape(dec_batch, dec_seq, SWA_KV_HEADS, SWA_HEAD_DIM)
    wb = cache_win_k.shape[2]
    s_wk = jnp.concatenate([cache_win_k[0].astype(F32), k1s], axis=1)[:, -wb:][None]
    s_wv = jnp.concatenate([cache_win_v[0].astype(F32), v1s], axis=1)[:, -wb:][None]
    return (y_prompt.reshape(batch, seq, d), y_sample.reshape(dec_batch, dec_seq, d),
            p_k, p_v, s_k, s_v, p_conv, s_conv, p_ssm[None], s_ssm[None], p_wk, p_wv, s_wk, s_wv)
```

```python
import functools
import math

import jax
import jax.numpy as jnp
from jax import lax
from jax.experimental import pallas as pl
from jax.experimental.pallas import tpu as pltpu

F32 = jnp.float32
BF16 = jnp.bfloat16

SSD_HEAD_DIM = 64
SSD_GROUPS = 2
SSD_STATE = 128
SSD_CONV = 4
SSD_CHUNK = 128
DIFF_HEAD_DIM = 64
DIFF_KV_HEADS = 4
SWA_HEAD_DIM = 64
SWA_KV_HEADS = 4
WINDOW = 128
ROPE_THETA = 500000.0
ROPE_ROT = 16
EPS = 1e-6
LOG2E = math.log2(math.e)

LANES = 128
HEAD = 64
VMEM_LIMIT = 56 * 1024 * 1024


def _dot(a, b):
    return jnp.dot(a, b, preferred_element_type=F32)


def _dot_nt(a, b):
    return lax.dot_general(a, b, (((1,), (1,)), ((), ())), preferred_element_type=F32)


def _split2_dot(v, m):
    hi = v.astype(BF16)
    lo = (v - hi.astype(F32)).astype(BF16)
    return _dot(hi, m) + _dot(lo, m)


def _split3_dot(m, v):
    hi = v.astype(BF16)
    r1 = v - hi.astype(F32)
    mid = r1.astype(BF16)
    lo = (r1 - mid.astype(F32)).astype(BF16)
    return _dot(m, hi) + _dot(m, mid) + _dot(m, lo)


def _silu(x):
    return x * (1.0 / (1.0 + jnp.exp(-x)))


def _softplus(x):
    return jnp.maximum(x, 0.0) + jnp.log(1.0 + jnp.exp(-jnp.abs(x)))


def _lane_half(shape):
    return (lax.broadcasted_iota(jnp.int32, shape, len(shape) - 1) % LANES) // HEAD


def _params(sem, vmem=VMEM_LIMIT):
    return pltpu.CompilerParams(dimension_semantics=sem, vmem_limit_bytes=vmem)


def _const_spec(shape):
    nd = len(shape)
    return pl.BlockSpec(shape, lambda *_: (0,) * nd)


def _row_specs(src, tm, n_p):
    if isinstance(src, tuple):
        p, s = src
        wp = p.shape[1]
        return [pl.BlockSpec((tm, wp), lambda i, *_: (jnp.minimum(i, n_p - 1), 0)),
                pl.BlockSpec((tm, wp), lambda i, *_: (jnp.maximum(i - n_p, 0), 0))], [p, s]
    return [pl.BlockSpec((tm, src.shape[1]), lambda i, *_: (i, 0))], [src]


def _row_load(refs, i, n_p, dtype):
    if len(refs) == 2:
        return jnp.where(i < n_p, refs[0][...].astype(dtype), refs[1][...].astype(dtype))
    return refs[0][...].astype(dtype)


def _rope_tables(seq, past_len, dec_seq, tm):
    half = ROPE_ROT // 2
    inv_freq = ROPE_THETA ** (-jnp.arange(half, dtype=F32) * (2.0 / ROPE_ROT))
    pos = jnp.concatenate([jnp.arange(seq, dtype=jnp.int32),
                           past_len + (jnp.arange(tm, dtype=jnp.int32) % dec_seq)]).astype(F32)
    ang = pos[:, None] * inv_freq[None, :]
    cos, sin = jnp.cos(ang), jnp.sin(ang)
    d = jnp.arange(LANES) % HEAD
    cos_l = jnp.where(d < ROPE_ROT, cos[:, d % half], 1.0)
    sin_a = jnp.where(d < half, -sin[:, d % half], 0.0)
    sin_b = jnp.where((d >= half) & (d < ROPE_ROT), sin[:, d % half], 0.0)
    return cos_l.astype(F32), sin_a.astype(F32), sin_b.astype(F32)


def _head_norm_rope(x, gain, bd, cos_l, sin_a, sin_b, scale):
    ms = _dot((x * x).astype(BF16), bd)
    y = x * lax.rsqrt(ms + EPS) * gain
    half = ROPE_ROT // 2
    y = y * cos_l + pltpu.roll(y, LANES - half, 1) * sin_a + pltpu.roll(y, half, 1) * sin_b
    if scale != 1.0:
        y = y * scale
    return y


def _norm_proj_kernel(*refs, n_src, n_p, segs, n_out):
    src = refs[:n_src]
    g_ref, cos_ref, sa_ref, sb_ref, bd_ref = refs[n_src:n_src + 5]
    pos = n_src + 5
    w_refs = refs[pos:pos + len(segs)]
    pos += len(segs)
    n_gain = sum(1 for s in segs if s["kind"] == "qk")
    gain_refs = refs[pos:pos + n_gain]
    pos += n_gain
    out_refs = refs[pos:pos + n_out]

    i = pl.program_id(0)
    x = _row_load(src, i, n_p, F32)
    xn = (x * lax.rsqrt(jnp.mean(x * x, axis=-1, keepdims=True) + EPS) * g_ref[...]).astype(BF16)

    oi = 0
    gi = 0
    tm = x.shape[0]
    head_rows = []
    for s, w_ref in zip(segs, w_refs):
        width = s["width"]
        n_refs = sum(2 if spec.get("rows") else 1 for spec in s["outs"])
        outs = out_refs[oi:oi + n_refs]
        oi += n_refs
        if s["kind"] == "qk":
            gain = gain_refs[gi][...]
            gi += 1
        step = min(width, 512)
        lane_wise = s["kind"] != "plain" or any(spec.get("rows") or spec.get("dup") for spec in s["outs"])
        for c0 in range(0, width, step):
            acc = _dot(xn, w_ref[:, c0:c0 + step])
            if not lane_wise:
                for o_ref, spec in zip(outs, s["outs"]):
                    o_ref[:, c0:c0 + step] = acc.astype(spec["dtype"])
                continue
            for l0 in range(0, step, LANES):
                y = acc[:, l0:l0 + LANES]
                if s["kind"] == "qk":
                    y = _head_norm_rope(y, gain, bd_ref[...], cos_ref[...], sa_ref[...], sb_ref[...], s["scale"])
                col = c0 + l0
                ri = 0
                for spec in s["outs"]:
                    o_ref = outs[ri]
                    if spec.get("rows"):
                        head_rows.append((outs[ri], outs[ri + 1], pl.ds(col // LANES, tm, stride=spec["rows"]), y))
                        ri += 2
                        continue
                    ri += 1
                    if spec.get("dup"):
                        sw = pltpu.roll(y, HEAD, 1)
                        first = _lane_half(y.shape) == 0
                        o_ref[:, 2 * col:2 * col + LANES] = jnp.where(first, y, sw).astype(spec["dtype"])
                        o_ref[:, 2 * col + LANES:2 * col + 2 * LANES] = jnp.where(first, sw, y).astype(spec["dtype"])
                    else:
                        o_ref[:, col:col + LANES] = y.astype(spec["dtype"])

    if head_rows:
        @pl.when(i < n_p)
        def _():
            for op_ref, _, tgt, y in head_rows:
                op_ref[tgt, :] = y.astype(op_ref.dtype)

        @pl.when(i >= n_p)
        def _():
            for _, os_ref, tgt, y in head_rows:
                os_ref[tgt, :] = y.astype(os_ref.dtype)


def _norm_proj(src, gain, segs, rope, bd, tm, n_p, n_t):
    d_model = gain.shape[-1]
    row_specs, row_args = _row_specs(src, tm, n_p)
    cos_l, sin_a, sin_b = rope
    n_pb = (cos_l.shape[0] - tm) // tm
    rope_spec = pl.BlockSpec((tm, LANES), lambda i: (jnp.where(i < n_p, i % n_pb, n_pb), 0))
    in_specs = row_specs + [_const_spec((1, d_model)), rope_spec, rope_spec, rope_spec, _const_spec((LANES, LANES))]
    args = row_args + [gain.reshape(1, d_model), cos_l, sin_a, sin_b, bd]
    for s in segs:
        in_specs.append(_const_spec(s["w"].shape))
        args.append(s["w"])
    for s in segs:
        if s["kind"] == "qk":
            in_specs.append(_const_spec((1, LANES)))
            args.append(jnp.tile(s["gain"].astype(F32), LANES // HEAD).reshape(1, LANES))
    out_shapes, out_specs = [], []
    for s in segs:
        for spec in s["outs"]:
            if spec.get("rows"):
                nr = spec["rows"]
                assert s["width"] == nr * LANES
                out_shapes += [jax.ShapeDtypeStruct((n_p * tm * nr, LANES), spec["dtype"]),
                               jax.ShapeDtypeStruct(((n_t - n_p) * tm * nr, LANES), spec["dtype"])]
                out_specs += [pl.BlockSpec((tm * nr, LANES), lambda i: (jnp.minimum(i, n_p - 1), 0)),
                              pl.BlockSpec((tm * nr, LANES), lambda i: (jnp.maximum(i - n_p, 0), 0))]
                continue
            w = s["width"] * (2 if spec.get("dup") else 1)
            out_shapes.append(jax.ShapeDtypeStruct((n_t * tm, w), spec["dtype"]))
            out_specs.append(pl.BlockSpec((tm, w), lambda i: (i, 0)))
    kern = functools.partial(
        _norm_proj_kernel, n_src=len(row_args), n_p=n_p,
        segs=[{k: v for k, v in s.items() if k not in ("w", "gain")} for s in segs], n_out=len(out_shapes))
    return pl.pallas_call(
        kern, grid=(n_t,), in_specs=in_specs, out_specs=out_specs, out_shape=out_shapes,
        compiler_params=_params(("arbitrary",)), name="norm_proj")(*args)


def _conv_rolls(u, prevpad, w4, bias, l_sub):
    rows = lax.broadcasted_iota(jnp.int32, u.shape, 0) % l_sub
    n = u.shape[0]
    y = bias + w4[SSD_CONV - 1:SSD_CONV, :] * u
    for k in range(SSD_CONV - 1):
        j = SSD_CONV - 1 - k
        cur = pltpu.roll(u, j, 0)
        prv = pltpu.roll(prevpad, n + j - 8, 0)
        y = y + w4[k:k + 1, :] * jnp.where(rows >= j, cur, prv)
    return y


def _ssd_block(xs_c, bm, cm, dt_raw, z, dtb, a_log, d_x, nw, eexp, hprev_fn, hstore_fn, l_sub):
    n = xs_c.shape[0]
    n_seq = n // l_sub
    hpg = xs_c.shape[1] // SSD_HEAD_DIM
    row = lax.broadcasted_iota(jnp.int32, (n, n), 0)
    col = lax.broadcasted_iota(jnp.int32, (n, n), 1)
    same = (row // l_sub) == (col // l_sub)
    tri = same & (col <= row)

    dt = _softplus(dt_raw + dtb)
    a = dt * (-jnp.exp(a_log))
    a_cum = _split3_dot(jnp.where(tri, 1.0, 0.0).astype(BF16), a)
    a_tot = _split3_dot(jnp.where(same, 1.0, 0.0).astype(BF16), a)
    a_cum_t = a_cum.T
    e_tot_t = jnp.exp(a_tot.T)

    xdt = xs_c * _split2_dot(dt, eexp)
    bm_b = bm.astype(BF16)
    cm_b = cm.astype(BF16)
    g_cb = _dot_nt(cm_b, bm_b)

    half = _lane_half((n, LANES))
    y_parts = []
    for pr in range(hpg // 2):
        ms = []
        for hh in range(2):
            h = 2 * pr + hh
            diff = a_cum[:, h:h + 1] - a_cum_t[h:h + 1, :]
            ms.append((g_cb * jnp.exp(jnp.where(tri, diff, -jnp.inf))).astype(BF16))
        xp = xdt[:, pr * LANES:(pr + 1) * LANES]
        xcat = jnp.concatenate([jnp.where(half == 0, xp, 0.0), jnp.where(half == 1, xp, 0.0)], axis=0).astype(BF16)
        y_parts.append(_dot(jnp.concatenate(ms, axis=1), xcat))
    y = jnp.concatenate(y_parts, axis=1)

    xw_t = (xdt * _split2_dot(jnp.exp(a_tot - a_cum), eexp)).T.astype(BF16)
    seq_row = lax.broadcasted_iota(jnp.int32, (n, 1), 0) // l_sub
    y_off = jnp.zeros_like(y)
    for b in range(n_seq):
        h_b = hprev_fn(b)
        r_b = _dot_nt(cm_b, h_b.reshape(hpg * SSD_HEAD_DIM, SSD_STATE).astype(BF16))
        bm_sel = bm_b if n_seq == 1 else jnp.where(seq_row == b, bm, 0.0).astype(BF16)
        s_b = _dot(xw_t, bm_sel)
        y_off = r_b if n_seq == 1 else jnp.where(seq_row == b, r_b, y_off)
        t0 = b * l_sub
        for h in range(hpg):
            hstore_fn(b, h, e_tot_t[h:h + 1, t0:t0 + 1] * h_b[h] + s_b[h * SSD_HEAD_DIM:(h + 1) * SSD_HEAD_DIM, :])
    y = y + y_off * _split2_dot(jnp.exp(a_cum), eexp) + xs_c * d_x

    gated = y * _silu(z.astype(F32))
    return (gated * lax.rsqrt(jnp.mean(gated * gated, axis=-1, keepdims=True) + EPS) * nw).astype(BF16)


def _ssd_prompt_kernel(xs_ref, b_ref, c_ref, hxs_ref, hb_ref, hc_ref, dt_ref, z_ref,
                       wxs_ref, wb_ref, wc_ref, bxs_ref, bb_ref, bc_ref,
                       dtb_ref, alog_ref, dx_ref, nw_ref, eexp_ref, y_ref, st_ref):
    c = pl.program_id(2)

    @pl.when(c == 0)
    def _():
        st_ref[...] = jnp.zeros_like(st_ref)

    keep = jnp.where(c > 0, 1.0, 0.0)

    def conv(u_ref, h_ref, w_ref, bias_ref):
        u = u_ref[...]
        prevpad = jnp.concatenate([h_ref[...] * keep, jnp.zeros((u.shape[0] - 8, u.shape[1]), F32)], axis=0)
        return _silu(_conv_rolls(u, prevpad, w_ref[...], bias_ref[...], u.shape[0]))

    xs_c = conv(xs_ref, hxs_ref, wxs_ref, bxs_ref)
    bm = conv(b_ref, hb_ref, wb_ref, bb_ref)
    cm = conv(c_ref, hc_ref, wc_ref, bc_ref)

    def hprev(b):
        return st_ref[0]

    def hstore(b, h, val):
        st_ref[0, h] = val

    y_ref[...] = _ssd_block(xs_c, bm, cm, dt_ref[...], z_ref[...], dtb_ref[...], alog_ref[...], dx_ref[...],
                            nw_ref[...], eexp_ref[...], hprev, hstore, xs_c.shape[0])


def _ssd_sample_kernel(xs_ref, b_ref, c_ref, pxs_ref, pb_ref, pc_ref, dt_ref, z_ref,
                       wxs_ref, wb_ref, wc_ref, bxs_ref, bb_ref, bc_ref,
                       dtb_ref, alog_ref, dx_ref, nw_ref, eexp_ref, st_in_ref, y_ref, st_ref, *, l_sub):
    def conv(u_ref, p_ref, w_ref, bias_ref):
        return _silu(_conv_rolls(u_ref[...], p_ref[...], w_ref[...], bias_ref[...], l_sub))

    xs_c = conv(xs_ref, pxs_ref, wxs_ref, bxs_ref)
    bm = conv(b_ref, pb_ref, wb_ref, bb_ref)
    cm = conv(c_ref, pc_ref, wc_ref, bc_ref)

    def hprev(b):
        return st_in_ref[b]

    def hstore(b, h, val):
        st_ref[b, h] = val

    y_ref[...] = _ssd_block(xs_c, bm, cm, dt_ref[...], z_ref[...], dtb_ref[...], alog_ref[...], dx_ref[...],
                            nw_ref[...], eexp_ref[...], hprev, hstore, l_sub)


def _ssd_common_specs(xw, gw, sw, row_of, ng):
    nb = xw // LANES
    def rs(w, colf):
        return pl.BlockSpec((SSD_CHUNK, w), lambda *g: (row_of(*g), colf(g[1])))
    return dict(
        xs=rs(gw, lambda g: g), b=rs(sw, lambda g: nb + g), c=rs(sw, lambda g: nb + ng + g),
        wxs=pl.BlockSpec((SSD_CONV, gw), lambda *g: (0, g[1])),
        wb=pl.BlockSpec((SSD_CONV, sw), lambda *g: (0, nb + g[1])),
        wc=pl.BlockSpec((SSD_CONV, sw), lambda *g: (0, nb + ng + g[1])),
        bxs=pl.BlockSpec((1, gw), lambda *g: (0, g[1])),
        bb=pl.BlockSpec((1, sw), lambda *g: (0, nb + g[1])),
        bc=pl.BlockSpec((1, sw), lambda *g: (0, nb + ng + g[1])),
        lane=pl.BlockSpec((1, LANES), lambda *g: (0, g[1])),
        grp=pl.BlockSpec((1, gw), lambda *g: (0, g[1])),
        eexp=_const_spec((LANES, gw)),
    )


def _ssd_prompt(xbc, dt, z, conv_w, conv_b, dtb, alog, d_x, nw, eexp, batch, seq):
    ng = SSD_GROUPS
    xw = z.shape[1]
    gw, sw = xw // ng, SSD_STATE
    hpg = gw // SSD_HEAD_DIM
    nc = seq // SSD_CHUNK
    row_of = lambda b, g, c: b * nc + c
    sp = _ssd_common_specs(xw, gw, sw, row_of, ng)
    nb = xw // LANES
    sub = SSD_CHUNK // 8

    def halo(w, colf):
        return pl.BlockSpec((8, w), lambda b, g, c: (jnp.maximum((b * nc + c) * sub - 1, 0), colf(g)))

    in_specs = [sp["xs"], sp["b"], sp["c"],
                halo(gw, lambda g: g), halo(sw, lambda g: nb + g), halo(sw, lambda g: nb + ng + g),
                pl.BlockSpec((SSD_CHUNK, LANES), lambda b, g, c: (row_of(b, g, c), g)),
                pl.BlockSpec((SSD_CHUNK, gw), lambda b, g, c: (row_of(b, g, c), g)),
                sp["wxs"], sp["wb"], sp["wc"], sp["bxs"], sp["bb"], sp["bc"],
                sp["lane"], sp["lane"], sp["grp"], sp["grp"], sp["eexp"]]
    out_specs = [pl.BlockSpec((SSD_CHUNK, gw), lambda b, g, c: (row_of(b, g, c), g)),
                 pl.BlockSpec((1, hpg, SSD_HEAD_DIM, SSD_STATE), lambda b, g, c: (b, g, 0, 0))]
    out_shape = [jax.ShapeDtypeStruct((batch * seq, xw), BF16),
                 jax.ShapeDtypeStruct((batch, ng * hpg, SSD_HEAD_DIM, SSD_STATE), F32)]
    return pl.pallas_call(
        _ssd_prompt_kernel, grid=(batch, ng, nc), in_specs=in_specs, out_specs=out_specs, out_shape=out_shape,
        compiler_params=_params(("arbitrary", "arbitrary", "arbitrary")), name="ssd_prompt")(
            xbc, xbc, xbc, xbc, xbc, xbc, dt, z, conv_w, conv_w, conv_w, conv_b, conv_b, conv_b,
            dtb, alog, d_x, nw, eexp)


def _ssd_sample(xbc, prev8, dt, z, conv_w, conv_b, dtb, alog, d_x, nw, eexp, state, t_p, dec_batch, dec_seq):
    ng = SSD_GROUPS
    xw = z.shape[1]
    gw, sw = xw // ng, SSD_STATE
    hpg = gw // SSD_HEAD_DIM
    n_seq = SSD_CHUNK // dec_seq
    nblk = dec_batch // n_seq
    rb0 = t_p // SSD_CHUNK
    row_of = lambda i, g: rb0 + i
    sp = _ssd_common_specs(xw, gw, sw, row_of, ng)
    nb = xw // LANES

    def prev(w, colf):
        return pl.BlockSpec((SSD_CHUNK, w), lambda i, g: (i, colf(g)))

    st_spec = pl.BlockSpec((n_seq, hpg, SSD_HEAD_DIM, SSD_STATE), lambda i, g: (i, g, 0, 0))
    in_specs = [sp["xs"], sp["b"], sp["c"],
                prev(gw, lambda g: g), prev(sw, lambda g: nb + g), prev(sw, lambda g: nb + ng + g),
                pl.BlockSpec((SSD_CHUNK, LANES), lambda i, g: (rb0 + i, g)),
                pl.BlockSpec((SSD_CHUNK, gw), lambda i, g: (rb0 + i, g)),
                sp["wxs"], sp["wb"], sp["wc"], sp["bxs"], sp["bb"], sp["bc"],
                sp["lane"], sp["lane"], sp["grp"], sp["grp"], sp["eexp"], st_spec]
    out_specs = [pl.BlockSpec((SSD_CHUNK, gw), lambda i, g: (i, g)), st_spec]
    out_shape = [jax.ShapeDtypeStruct((dec_batch * dec_seq, xw), BF16),
                 jax.ShapeDtypeStruct(state.shape, F32)]
    return pl.pallas_call(
        functools.partial(_ssd_sample_kernel, l_sub=dec_seq), grid=(nblk, ng), in_specs=in_specs,
        out_specs=out_specs, out_shape=out_shape,
        compiler_params=_params(("arbitrary", "arbitrary")), name="ssd_sample")(
            xbc, xbc, xbc, prev8, prev8, prev8, dt, z, conv_w, conv_w, conv_w, conv_b, conv_b, conv_b,
            dtb, alog, d_x, nw, eexp, state)


def _lambda(lq1, lk1, lq2, lk2, lam_init):
    s1 = jnp.sum(lq1[...] * lk1[...], axis=-1, keepdims=True)
    s2 = jnp.sum(lq2[...] * lk2[...], axis=-1, keepdims=True)
    return jnp.exp(s1) - jnp.exp(s2) + lam_init


def _subln(att, w, lam_init):
    return att * lax.rsqrt(jnp.mean(att * att, axis=-1, keepdims=True) + EPS) * w * (1.0 - lam_init)


def _lane_tile(x, n):
    return x if n == 1 else jnp.concatenate([x] * n, axis=1)


def _diff_prompt_kernel(q_ref, k_ref, v_ref, lq1, lk1, lq2, lk2, sub_ref, o_ref,
                        lhs_ref, vext_ref, s_ref, m_ref, acc_ref, *, tq, tk, lam_init):
    qb = pl.program_id(2)
    n_g = q_ref.shape[1] // LANES
    rows = 2 * n_g * tq
    nl = tk // LANES

    @pl.when(qb == 0)
    def _():
        vext_ref[:, :LANES] = v_ref[...]
        vext_ref[:, LANES:] = jnp.ones((v_ref.shape[0], LANES), vext_ref.dtype)

    half = _lane_half((tq, LANES))
    for j in range(2):
        for g in range(n_g):
            qg = q_ref[:, g * LANES:(g + 1) * LANES]
            r0 = (j * n_g + g) * tq
            lhs_ref[r0:r0 + tq, :] = jnp.where(half == j, qg, jnp.zeros_like(qg))
    m_ref[...] = jnp.full(m_ref.shape, -jnp.inf, F32)
    acc_ref[...] = jnp.zeros(acc_ref.shape, F32)
    n_t = (qb * tq) // tk + 1

    def scores(t, masked):
        k = k_ref[pl.ds(pl.multiple_of(t * tk, tk), tk), :]
        s = _dot_nt(lhs_ref[...], k)
        if masked:
            qpos = qb * tq + lax.broadcasted_iota(jnp.int32, (rows, tk), 0) % tq
            kpos = t * tk + lax.broadcasted_iota(jnp.int32, (rows, tk), 1)
            s = jnp.where(kpos <= qpos, s, -jnp.inf)
        s_ref[t] = s
        mt = s[:, :LANES]
        for c in range(1, nl):
            mt = jnp.maximum(mt, s[:, c * LANES:(c + 1) * LANES])
        m_ref[...] = jnp.maximum(m_ref[...], mt)

    lax.fori_loop(0, n_t - 1, lambda t, c: (scores(t, False), c)[1], 0)
    scores(n_t - 1, True)
    m_ref[...] = jnp.broadcast_to(jnp.max(m_ref[...], axis=-1, keepdims=True), m_ref.shape)

    def attend(t, c):
        p = jnp.exp2(s_ref[t] - _lane_tile(m_ref[...], nl)).astype(BF16)
        acc_ref[...] += _dot(p, vext_ref[pl.ds(pl.multiple_of(t * tk, tk), tk), :])
        return c

    lax.fori_loop(0, n_t, attend, 0)

    lam = _lambda(lq1, lk1, lq2, lk2, lam_init)
    acc = acc_ref[...]
    o = acc[:, :LANES] / acc[:, LANES:]
    att = o[:rows // 2] - lam * o[rows // 2:]
    res = _subln(att, sub_ref[...], lam_init)
    for g in range(n_g):
        o_ref[:, g * LANES:(g + 1) * LANES] = res[g * tq:(g + 1) * tq].astype(o_ref.dtype)


def _diff_prompt(q, kb, vb, lam_vecs, subln, batch, seq, lam_init):
    t_p = batch * seq
    n_kvh = DIFF_KV_HEADS
    qw = q.shape[1] // n_kvh
    kw = kb.shape[1] // n_kvh
    tq = min(512, seq)
    tk = min(512, seq)
    assert seq % tk == 0 and tk % tq == 0 and kw == LANES
    nq = seq // tq
    rows = 2 * (qw // LANES) * tq
    vec = _const_spec((1, DIFF_HEAD_DIM))
    in_specs = [pl.BlockSpec((tq, qw), lambda b, h, i: (b * nq + i, h)),
                pl.BlockSpec((seq, kw), lambda b, h, i: (b, h)),
                pl.BlockSpec((seq, kw), lambda b, h, i: (b, h)),
                vec, vec, vec, vec, _const_spec((1, LANES))]
    return pl.pallas_call(
        functools.partial(_diff_prompt_kernel, tq=tq, tk=tk, lam_init=lam_init),
        grid=(batch, n_kvh, nq), in_specs=in_specs,
        out_specs=pl.BlockSpec((tq, qw), lambda b, h, i: (b * nq + i, h)),
        out_shape=jax.ShapeDtypeStruct((t_p, q.shape[1]), BF16),
        scratch_shapes=[pltpu.VMEM((rows, LANES), BF16), pltpu.VMEM((seq, 2 * LANES), BF16),
                        pltpu.VMEM((seq // tk, rows, tk), F32), pltpu.VMEM((rows, LANES), F32),
                        pltpu.VMEM((rows, 2 * LANES), F32)],
        compiler_params=_params(("arbitrary", "arbitrary", "arbitrary")), name="diff_attn_prompt")(
            q, kb, vb, *lam_vecs, subln)


def _diff_paged_kernel(pt_ref, q_ref, kn_ref, vn_ref, ck_hbm, cv_hbm, lq1, lk1, lq2, lk2, sub_ref, o_ref,
                       kbuf, vbuf, sem, qst_ref, bias_ref, m_ref, acc_ref, *, n_pg, pg_stage, dec_seq, lam_init):
    p = pl.program_id(1)
    n_steps = pl.num_programs(1)
    step = pl.program_id(0) * n_steps + p
    last_step = pl.num_programs(0) * n_steps - 1
    slot = step % 2
    n_kvh = DIFF_KV_HEADS
    n_g = q_ref.shape[1] // (n_kvh * LANES)
    rph = n_g * 2 * dec_seq
    rows = n_kvh * rph
    prow = kbuf.shape[2]

    def page_copies(s, sl, lookup):
        bb, pp = s // n_steps, s % n_steps
        out = []
        for i in range(n_pg):
            pg = pt_ref[bb, pp * n_pg + i] if lookup else 0
            out.append(pltpu.make_async_copy(ck_hbm.at[pg], kbuf.at[sl, i], sem.at[sl]))
            out.append(pltpu.make_async_copy(cv_hbm.at[pg], vbuf.at[sl, i], sem.at[sl]))
        return out

    def start_all(copies):
        for n, c in enumerate(copies):
            c.start(priority=n % 2)

    @pl.when(step == 0)
    def _():
        start_all(page_copies(step, slot, True))

    @pl.when(step < last_step)
    def _():
        start_all(page_copies(step + 1, 1 - slot, True))

    for c in page_copies(step, slot, False):
        c.wait()
    k_pages = [kbuf.at[slot, i] for i in range(n_pg)]
    v_pages = [vbuf.at[slot, i] for i in range(n_pg)]

    @pl.when(p == 0)
    def _():
        half = _lane_half((dec_seq, LANES))
        parts = []
        for h in range(n_kvh):
            for g in range(n_g):
                qg = q_ref[:, (h * n_g + g) * LANES:(h * n_g + g + 1) * LANES]
                for j in range(2):
                    parts.append(jnp.where(half == j, qg, 0.0))
        qst_ref[...] = jnp.concatenate(parts, axis=0).astype(BF16)
        head_r = lax.broadcasted_iota(jnp.int32, (rows, prow), 0) // rph
        head_c = lax.broadcasted_iota(jnp.int32, (rows, prow), 1) % n_kvh
        bias_ref[...] = jnp.where(head_r == head_c, 0.0, -jnp.inf)
        m_ref[...] = jnp.full(m_ref.shape, -jnp.inf, F32)
        acc_ref[...] = jnp.zeros(acc_ref.shape, F32)

    def update(ss, bias, vals, lane_ok):
        mt = None
        for s in ss:
            for c in range(s.shape[1] // LANES):
                blk = s[:, c * LANES:(c + 1) * LANES]
                mt = blk if mt is None else jnp.maximum(mt, blk)
        m_old = m_ref[...]
        m_new = jnp.maximum(m_old, jnp.max(jnp.where(lane_ok, mt, -jnp.inf), axis=-1, keepdims=True))
        corr = jnp.exp2(m_old - m_new)
        shift = bias - _lane_tile(m_new, bias.shape[1] // LANES)
        pv = None
        for s, v in zip(ss, vals):
            pr = jnp.exp2(s + _lane_tile(shift, s.shape[1] // bias.shape[1])).astype(BF16)
            vext = jnp.concatenate([v.astype(BF16), jnp.ones(v.shape, BF16)], axis=1)
            t = _dot(pr, vext)
            pv = t if pv is None else pv + t
        acc_ref[...] = acc_ref[...] * _lane_tile(corr, 2) + pv
        m_ref[...] = m_new

    qst = qst_ref[...]
    own_lane = (lax.broadcasted_iota(jnp.int32, (rows, LANES), 1) % n_kvh
                == lax.broadcasted_iota(jnp.int32, (rows, LANES), 0) // rph)
    for i0 in range(0, n_pg, pg_stage):
        grp = range(i0, min(i0 + pg_stage, n_pg))
        update([_dot_nt(qst, k_pages[i][...].astype(BF16)) for i in grp], bias_ref[...],
               [v_pages[i][...] for i in grp], own_lane)

    @pl.when(p == pl.num_programs(1) - 1)
    def _():
        lam = _lambda(lq1, lk1, lq2, lk2, lam_init)
        pad = jnp.zeros((LANES - n_kvh * dec_seq, LANES), F32)
        kn = jnp.concatenate([kn_ref[...], pad], axis=0)
        vn = jnp.concatenate([vn_ref[...], pad], axis=0)
        ri = lax.broadcasted_iota(jnp.int32, (rows, LANES), 0)
        ci = lax.broadcasted_iota(jnp.int32, (rows, LANES), 1)
        ok = (ci % n_kvh == ri // rph) & (ci // n_kvh <= ri % dec_seq)
        update([_dot_nt(qst, kn.astype(BF16))], jnp.where(ok, 0.0, -jnp.inf), [vn], ok)
        acc = acc_ref[...]
        o = acc[:, :LANES] / acc[:, LANES:]
        for h in range(n_kvh):
            for g in range(n_g):
                r0 = h * rph + g * 2 * dec_seq
                att = o[r0:r0 + dec_seq] - lam * o[r0 + dec_seq:r0 + 2 * dec_seq]
                c0 = (h * n_g + g) * LANES
                o_ref[:, c0:c0 + LANES] = _subln(att, sub_ref[...], lam_init).astype(o_ref.dtype)


def _diff_paged(q_s, k_new, v_new, cache_k, cache_v, page_table, lam_vecs, subln, dec_batch, dec_seq, lam_init):
    n_pages = page_table.shape[1]
    n_pg = math.gcd(n_pages, 8)
    n_kvh = DIFF_KV_HEADS
    page_rows = cache_k.shape[1]
    n_g = q_s.shape[1] // (n_kvh * LANES)
    rows = n_kvh * n_g * 2 * dec_seq
    nr = n_kvh * dec_seq
    assert nr <= LANES and k_new.shape == (dec_batch * nr, LANES)
    vec = _const_spec((1, DIFF_HEAD_DIM))

    hbm = pl.BlockSpec(memory_space=pl.ANY)
    in_specs = [pl.BlockSpec((dec_seq, q_s.shape[1]), lambda b, p, pt: (b, 0)),
                pl.BlockSpec((nr, LANES), lambda b, p, pt: (b, 0)),
                pl.BlockSpec((nr, LANES), lambda b, p, pt: (b, 0)),
                hbm, hbm, vec, vec, vec, vec, _const_spec((1, LANES))]
    grid_spec = pltpu.PrefetchScalarGridSpec(
        num_scalar_prefetch=1, grid=(dec_batch, n_pages // n_pg), in_specs=in_specs,
        out_specs=pl.BlockSpec((dec_seq, q_s.shape[1]), lambda b, p, pt: (b, 0)),
        scratch_shapes=[pltpu.VMEM((2, n_pg, page_rows, LANES), cache_k.dtype),
                        pltpu.VMEM((2, n_pg, page_rows, LANES), cache_v.dtype),
                        pltpu.SemaphoreType.DMA((2,)),
                        pltpu.VMEM((rows, LANES), BF16), pltpu.VMEM((rows, page_rows), F32),
                        pltpu.VMEM((rows, LANES), F32), pltpu.VMEM((rows, 2 * LANES), F32)])
    return pl.pallas_call(
        functools.partial(_diff_paged_kernel, n_pg=n_pg, pg_stage=min(1, n_pg), dec_seq=dec_seq, lam_init=lam_init),
        grid_spec=grid_spec, out_shape=jax.ShapeDtypeStruct(q_s.shape, F32),
        compiler_params=_params(("arbitrary", "arbitrary")), name="diff_attn_paged")(
            page_table, q_s, k_new, v_new, cache_k, cache_v, *lam_vecs, subln)


def _sink_attend(lhs, blocks, sink):
    ss = [_dot_nt(lhs, k2) + bias for k2, _, bias in blocks]
    mt = None
    for s in ss:
        for c in range(s.shape[1] // LANES):
            blk = s[:, c * LANES:(c + 1) * LANES]
            mt = blk if mt is None else jnp.maximum(mt, blk)
    m = jnp.maximum(jnp.max(mt, axis=-1, keepdims=True), sink)
    acc = None
    for s, (_, v2, _) in zip(ss, blocks):
        p = jnp.exp2(s - _lane_tile(m, s.shape[1] // LANES)).astype(BF16)
        t = _dot(p, jnp.concatenate([v2, jnp.ones(v2.shape, BF16)], axis=1))
        acc = t if acc is None else acc + t
    return acc[:, :LANES] / (acc[:, LANES:] + jnp.exp2(sink - m))


def _swa_prompt_kernel(q_ref, kc_ref, kp_ref, vc_ref, vp_ref, sink_ref, o_ref):
    n = pl.program_id(1)
    w = q_ref.shape[0]
    n_kvh = SWA_KV_HEADS
    n_g = q_ref.shape[1] // (n_kvh * HEAD)
    half = _lane_half((w, LANES))
    qi = lax.broadcasted_iota(jnp.int32, (w, 2 * w), 0)
    ci = lax.broadcasted_iota(jnp.int32, (w, 2 * w), 1)
    ok = (ci > qi) & (ci <= qi + w) & ((ci >= w) | (n > 0))
    bias1 = jnp.where(ok, 0.0, -jnp.inf)
    bias = jnp.concatenate([bias1] * n_g, axis=0)
    sink_all = sink_ref[...] * LOG2E
    for h in range(n_kvh):
        cols = slice(h * LANES, (h + 1) * LANES)
        k2 = jnp.concatenate([kp_ref[:, cols], kc_ref[:, cols]], axis=0)
        v2 = jnp.concatenate([vp_ref[:, cols], vc_ref[:, cols]], axis=0)
        parts, sinks = [], []
        for g in range(n_g):
            hd = h * n_g + g
            qg = q_ref[:, (hd // 2) * LANES:(hd // 2 + 1) * LANES]
            parts.append(jnp.where(half == hd % 2, qg, jnp.zeros_like(qg)))
            sinks.append(jnp.broadcast_to(sink_all[0:1, hd:hd + 1], (w, LANES)))
        o2 = _sink_attend(jnp.concatenate(parts, axis=0), [(k2, v2, bias)], jnp.concatenate(sinks, axis=0))
        for a in range(n_g // 2):
            hd = h * n_g + 2 * a
            pair = jnp.where(half == 0, o2[2 * a * w:(2 * a + 1) * w], o2[(2 * a + 1) * w:(2 * a + 2) * w])
            o_ref[:, (hd // 2) * LANES:(hd // 2 + 1) * LANES] = pair.astype(o_ref.dtype)


def _swa_prompt(q, k2, v2, sinks, batch, seq):
    w = WINDOW
    nb = seq // w
    kw = k2.shape[1]
    cur = pl.BlockSpec((w, kw), lambda b, n: (b * nb + n, 0))
    prv = pl.BlockSpec((w, kw), lambda b, n: (jnp.maximum(b * nb + n - 1, 0), 0))
    in_specs = [pl.BlockSpec((w, q.shape[1]), lambda b, n: (b * nb + n, 0)), cur, prv, cur, prv,
                _const_spec(sinks.shape)]
    return pl.pallas_call(
        _swa_prompt_kernel, grid=(batch, nb), in_specs=in_specs,
        out_specs=pl.BlockSpec((w, q.shape[1]), lambda b, n: (b * nb + n, 0)),
        out_shape=jax.ShapeDtypeStruct((batch * seq, q.shape[1]), BF16),
        compiler_params=_params(("arbitrary", "arbitrary")), name="swa_prompt")(q, k2, k2, v2, v2, sinks)


def _swa_sample_kernel(q_ref, wk_ref, wv_ref, kn_ref, vn_ref, sink_ref, o_ref, *, dec_seq):
    nbb, wb, kw = wk_ref.shape
    n_kvh = SWA_KV_HEADS
    n_h = q_ref.shape[1] // HEAD
    n_g = n_h // n_kvh
    rows = n_h * dec_seq
    nv = kw // LANES
    half8 = _lane_half((dec_seq, LANES))
    qi = lax.broadcasted_iota(jnp.int32, (rows, wb), 0) % dec_seq
    ci = lax.broadcasted_iota(jnp.int32, (rows, wb), 1)
    bias_c = jnp.where(ci > qi, 0.0, -jnp.inf)
    bias_n = jnp.where(ci <= qi, 0.0, -jnp.inf)
    pad = jnp.zeros((wb - dec_seq, kw), F32)
    zero8 = jnp.zeros((dec_seq, LANES), F32)
    sink_all = sink_ref[...] * LOG2E
    sink = jnp.concatenate([jnp.broadcast_to(sink_all[0:1, hd:hd + 1], (dec_seq, LANES)) for hd in range(n_h)], axis=0)

    for bb in range(nbb):
        r0 = bb * dec_seq
        q8 = q_ref[r0:r0 + dec_seq, :]
        pieces = []
        for hd in range(n_h):
            kvh = hd // n_g
            x = jnp.where(half8 == hd % 2, q8[:, (hd // 2) * LANES:(hd // 2 + 1) * LANES], 0.0)
            if hd % 2 != kvh % 2:
                x = pltpu.roll(x, HEAD, 1)
            pieces.append(jnp.concatenate([x if c == kvh // 2 else zero8 for c in range(nv)], axis=1))
        qbd = jnp.concatenate(pieces, axis=0).astype(BF16)
        kn = jnp.concatenate([kn_ref[r0:r0 + dec_seq, :], pad], axis=0).astype(BF16)
        vn = jnp.concatenate([vn_ref[r0:r0 + dec_seq, :], pad], axis=0).astype(BF16)
        s_c = _dot_nt(qbd, wk_ref[bb].astype(BF16)) + bias_c
        s_n = _dot_nt(qbd, kn) + bias_n
        m = jnp.maximum(jnp.max(jnp.maximum(s_c, s_n), axis=-1, keepdims=True), sink)
        p_c, p_n = jnp.exp2(s_c - m), jnp.exp2(s_n - m)
        den = jnp.sum(p_c + p_n, axis=-1, keepdims=True) + jnp.exp2(sink - m)
        o_all = (_dot(p_c.astype(BF16), wv_ref[bb].astype(BF16)) + _dot(p_n.astype(BF16), vn)) / _lane_tile(den, nv)
        for pr in range(n_h // 2):
            blks = []
            for e in range(2):
                hd = 2 * pr + e
                kvh = hd // n_g
                blk = o_all[hd * dec_seq:(hd + 1) * dec_seq, (kvh // 2) * LANES:(kvh // 2 + 1) * LANES]
                blks.append(pltpu.roll(blk, HEAD, 1) if kvh % 2 != e else blk)
            o_ref[r0:r0 + dec_seq, pr * LANES:(pr + 1) * LANES] = jnp.where(half8 == 0, blks[0], blks[1])


def _swa_sample(q_s, wk, wv, k_all, v_all, sinks, t_p, dec_batch, dec_seq):
    nbb = math.gcd(dec_batch, 8)
    rows = nbb * dec_seq
    rb0 = t_p // rows
    kw = k_all.shape[1]
    wb = wk.shape[1]
    in_specs = [pl.BlockSpec((rows, q_s.shape[1]), lambda i: (i, 0)),
                pl.BlockSpec((nbb, wb, kw), lambda i: (i, 0, 0)),
                pl.BlockSpec((nbb, wb, kw), lambda i: (i, 0, 0)),
                pl.BlockSpec((rows, kw), lambda i: (rb0 + i, 0)),
                pl.BlockSpec((rows, kw), lambda i: (rb0 + i, 0)),
                _const_spec(sinks.shape)]
    return pl.pallas_call(
        functools.partial(_swa_sample_kernel, dec_seq=dec_seq), grid=(dec_batch // nbb,), in_specs=in_specs,
        out_specs=pl.BlockSpec((rows, q_s.shape[1]), lambda i: (i, 0)),
        out_shape=jax.ShapeDtypeStruct(q_s.shape, F32),
        compiler_params=_params(("arbitrary",)), name="swa_sample")(q_s, wk, wv, k_all, v_all, sinks)


def _out_mlp_kernel(*refs, n_h, n_act, n_p, split_out):
    h_refs = refs[:n_h]
    pos = n_h
    act_refs = []
    for na in n_act:
        act_refs.append(refs[pos:pos + na])
        pos += na
    wo_ref, g_ref, wu_ref, wd_ref = refs[pos:pos + 4]
    pos += 4
    n_o = 2 if split_out else 1
    out_refs = refs[pos:pos + n_o]
    h1_ref, hn_ref, acc_ref = refs[pos + n_o:]
    i = pl.program_id(0)
    j = pl.program_id(1)
    d = h1_ref.shape[1]

    @pl.when(j == 0)
    def _():
        mix = None
        for s, ar in enumerate(act_refs):
            t = _dot(_row_load(ar, i, n_p, BF16), wo_ref[s * d:(s + 1) * d, :])
            mix = t if mix is None else mix + t
        h1 = _row_load(h_refs, i, n_p, F32) + mix
        h1_ref[...] = h1
        hn_ref[...] = (h1 * lax.rsqrt(jnp.mean(h1 * h1, axis=-1, keepdims=True) + EPS) * g_ref[...]).astype(BF16)
        acc_ref[...] = jnp.zeros_like(acc_ref)

    u = jnp.maximum(_dot(hn_ref[...], wu_ref[...]), 0.0)
    acc_ref[...] += _dot((u * u).astype(BF16), wd_ref[...])

    @pl.when(j == pl.num_programs(1) - 1)
    def _():
        res = h1_ref[...] + acc_ref[...]
        if split_out:
            @pl.when(i < n_p)
            def _():
                out_refs[0][...] = res

            @pl.when(i >= n_p)
            def _():
                out_refs[1][...] = res
        else:
            out_refs[0][...] = res


def _out_mlp(h_src, act_srcs, w_out, gain, w_up, w_down, tm, n_p, n_t, split_out):
    d = gain.shape[-1]
    d_ff = w_up.shape[1]
    tf = min(1024, d_ff)
    h_specs, h_args = _row_specs(h_src, tm, n_p)
    in_specs, args, n_act = list(h_specs), list(h_args), []
    for a in act_srcs:
        sp, ar = _row_specs(a, tm, n_p)
        in_specs += sp
        args += ar
        n_act.append(len(ar))
    in_specs += [_const_spec(w_out.shape), _const_spec((1, d)),
                 pl.BlockSpec((d, tf), lambda i, j: (0, j)), pl.BlockSpec((tf, d), lambda i, j: (j, 0))]
    args += [w_out, gain.reshape(1, d), w_up, w_down]
    if split_out:
        n_s = n_t - n_p
        out_shape = [jax.ShapeDtypeStruct((n_p * tm, d), F32), jax.ShapeDtypeStruct((n_s * tm, d), F32)]
        out_specs = [pl.BlockSpec((tm, d), lambda i, j: (jnp.minimum(i, n_p - 1), 0)),
                     pl.BlockSpec((tm, d), lambda i, j: (jnp.maximum(i - n_p, 0), 0))]
    else:
        out_shape = [jax.ShapeDtypeStruct((n_t * tm, d), F32)]
        out_specs = [pl.BlockSpec((tm, d), lambda i, j: (i, 0))]
    kern = functools.partial(_out_mlp_kernel, n_h=len(h_args), n_act=tuple(n_act), n_p=n_p, split_out=split_out)
    return pl.pallas_call(
        kern, grid=(n_t, d_ff // tf), in_specs=in_specs, out_specs=out_specs, out_shape=out_shape,
        scratch_shapes=[pltpu.VMEM((tm, d), F32), pltpu.VMEM((tm, d), BF16), pltpu.VMEM((tm, d), F32)],
        compiler_params=_params(("arbitrary", "arbitrary")), name="out_mlp")(*args)


def kernel(x_prompt, x_sample, cache_k, cache_v, page_table, state_conv, state_ssm, cache_win_k, cache_win_v,
           norm_mix, norm_mlp, w_up, w_down, a_w_in, a_conv_w, a_conv_b, a_dt_bias, a_A_log, a_D, a_ssd_norm,
           a_q_norm, a_k_norm, a_lam_q1, a_lam_k1, a_lam_q2, a_lam_k2, a_subln, a_w_out,
           c_w_in, c_q_norm, c_k_norm, c_sinks, c_w_out):
    batch, seq, d = x_prompt.shape
    dec_batch, dec_seq, _ = x_sample.shape
    page = cache_k.shape[2]
    past_len = page_table.shape[1] * page
    t_p, t_s = batch * seq, dec_batch * dec_seq
    assert a_w_in.shape[0] == 1 and c_w_in.shape[0] == 1 and norm_mix.shape[0] == 2, "kernel is written for depth 2"
    assert d % (2 * LANES) == 0 and seq % SSD_CHUNK == 0 and SSD_CHUNK % dec_seq == 0
    assert dec_batch % (SSD_CHUNK // dec_seq) == 0 and cache_win_k.shape[2] == WINDOW

    tm = next(t for t in (512, 256, 128) if t_p % t == 0 and t_s % t == 0 and seq % t == 0 and t % dec_seq == 0)
    n_p, n_t = t_p // tm, (t_p + t_s) // tm

    xp = x_prompt.reshape(t_p, d)
    xs = x_sample.reshape(t_s, d)
    rope = _rope_tables(seq, past_len, dec_seq, tm)
    ii = jnp.arange(LANES)
    bd = jnp.where((ii[:, None] // HEAD) == (ii[None, :] // HEAD), 1.0 / HEAD, 0.0).astype(BF16)

    xw = d
    cw = xw + 2 * SSD_GROUPS * SSD_STATE
    n_h = xw // SSD_HEAD_DIM
    hpg = n_h // SSD_GROUPS
    qw = d
    kvw = DIFF_KV_HEADS * 2 * DIFF_HEAD_DIM
    offs = [0, xw, xw + cw, xw + cw + n_h, xw + cw + n_h + qw, xw + cw + n_h + qw + kvw]
    w_in = a_w_in[0]
    w_z, w_xbc, w_dt, w_q, w_k, w_v = (w_in[:, offs[0]:offs[1]], w_in[:, offs[1]:offs[2]], w_in[:, offs[2]:offs[3]],
                                       w_in[:, offs[3]:offs[4]], w_in[:, offs[4]:offs[5]], w_in[:, offs[5]:])
    w_dt_g = jnp.pad(w_dt.reshape(d, SSD_GROUPS, hpg), ((0, 0), (0, 0), (0, LANES - hpg))).reshape(d, SSD_GROUPS * LANES)
    bf = lambda w: w.astype(BF16)
    segs0 = [
        dict(w=bf(w_z), kind="plain", width=xw, outs=[dict(dtype=BF16)]),
        dict(w=bf(w_xbc), kind="plain", width=cw, outs=[dict(dtype=F32)]),
        dict(w=bf(w_dt_g), kind="plain", width=SSD_GROUPS * LANES, outs=[dict(dtype=F32)]),
        dict(w=bf(w_q), kind="qk", width=qw, gain=a_q_norm[0], scale=DIFF_HEAD_DIM ** -0.5 * LOG2E, outs=[dict(dtype=BF16)]),
        dict(w=bf(w_k), kind="qk", width=kvw, gain=a_k_norm[0], scale=1.0,
             outs=[dict(dtype=F32, rows=DIFF_KV_HEADS), dict(dtype=BF16)]),
        dict(w=bf(w_v), kind="plain", width=kvw, outs=[dict(dtype=F32, rows=DIFF_KV_HEADS), dict(dtype=BF16)]),
    ]
    z, xbc, dt, q0, k_p, k_s, k0b, v_p, v_s, v0b = _norm_proj((xp, xs), norm_mix[0], segs0, rope, bd, tm, n_p, n_t)

    pad_l = lambda v: jnp.pad(v.astype(F32).reshape(SSD_GROUPS, hpg), ((0, 0), (0, LANES - hpg))).reshape(1, SSD_GROUPS * LANES)
    dtb, alog = pad_l(a_dt_bias[0]), pad_l(a_A_log[0])
    d_x = jnp.repeat(a_D[0].astype(F32), SSD_HEAD_DIM).reshape(1, xw)
    nw = a_ssd_norm[0].astype(F32).reshape(1, xw)
    gw = xw // SSD_GROUPS
    eexp = jnp.where(ii[:, None] == (jnp.arange(gw)[None, :] // SSD_HEAD_DIM), 1.0, 0.0).astype(BF16)
    conv_w, conv_b = a_conv_w[0].astype(F32), a_conv_b[0].astype(F32).reshape(1, cw)

    y_p, p_ssm = _ssd_prompt(xbc, dt, z, conv_w, conv_b, dtb, alog, d_x, nw, eexp, batch, seq)
    prev8 = jnp.pad(state_conv[0].astype(F32), ((0, 0), (8 - (SSD_CONV - 1), 0), (0, 0))).reshape(dec_batch * 8, cw)
    assert dec_seq == 8, "sample conv halo layout assumes 8-token sequences"
    y_s, s_ssm = _ssd_sample(xbc, prev8, dt, z, conv_w, conv_b, dtb, alog, d_x, nw, eexp,
                             state_ssm[0].astype(F32), t_p, dec_batch, dec_seq)

    lam_init0 = 0.8 - 0.6 * math.exp(-0.3 * 0)
    lam_vecs = [v[0].astype(F32).reshape(1, DIFF_HEAD_DIM) for v in (a_lam_q1, a_lam_k1, a_lam_q2, a_lam_k2)]
    subln = a_subln[0].astype(F32).reshape(1, LANES)
    o_p = _diff_prompt(q0, k0b, v0b, lam_vecs, subln, batch, seq, lam_init0)
    n_pool = cache_k.shape[1]
    o_s = _diff_paged(q0[t_p:].astype(F32), k_s, v_s, cache_k[0].reshape(n_pool, page * DIFF_KV_HEADS, LANES),
                      cache_v[0].reshape(n_pool, page * DIFF_KV_HEADS, LANES), page_table.astype(jnp.int32), lam_vecs, subln,
                      dec_batch, dec_seq, lam_init0)

    (h1,) = _out_mlp((xp, xs), [(y_p, y_s), (o_p, o_s)], bf(a_w_out[0]), norm_mlp[0], bf(w_up[0]), bf(w_down[0]),
                     tm, n_p, n_t, split_out=False)

    cqw = d
    ckw = SWA_KV_HEADS * SWA_HEAD_DIM
    wc = c_w_in[0]
    segs1 = [
        dict(w=bf(wc[:, :cqw]), kind="qk", width=cqw, gain=c_q_norm[0], scale=SWA_HEAD_DIM ** -0.5 * LOG2E, outs=[dict(dtype=BF16)]),
        dict(w=bf(wc[:, cqw:cqw + ckw]), kind="qk", width=ckw, gain=c_k_norm[0], scale=1.0,
             outs=[dict(dtype=F32), dict(dtype=BF16, dup=True)]),
        dict(w=bf(wc[:, cqw + ckw:]), kind="dupv", width=ckw, outs=[dict(dtype=F32), dict(dtype=BF16, dup=True)]),
    ]
    q1, k1, k1d, v1, v1d = _norm_proj(h1, norm_mix[1], segs1, rope, bd, tm, n_p, n_t)
    sinks = c_sinks[0].astype(F32).reshape(1, -1)
    o1_p = _swa_prompt(q1, k1d, v1d, sinks, batch, seq)
    o1_s = _swa_sample(q1[t_p:].astype(F32), cache_win_k[0].reshape(dec_batch, WINDOW, ckw),
                       cache_win_v[0].reshape(dec_batch, WINDOW, ckw), k1, v1, sinks, t_p, dec_batch, dec_seq)
    y_prompt, y_sample = _out_mlp(h1, [(o1_p, o1_s)], bf(c_w_out[0]), norm_mlp[1], bf(w_up[1]), bf(w_down[1]),
                                  tm, n_p, n_t, split_out=True)

    n_kv = DIFF_KV_HEADS
    p_k = k_p.reshape(1, batch, seq, n_kv, kvw // n_kv)
    p_v = v_p.reshape(1, batch, seq, n_kv, kvw // n_kv)
    s_k = k_s.reshape(1, dec_batch, dec_seq, n_kv, kvw // n_kv)
    s_v = v_s.reshape(1, dec_batch, dec_seq, n_kv, kvw // n_kv)
    keep = SSD_CONV - 1
    tail = lambda a, n: jnp.stack([a[(b + 1) * seq - n:(b + 1) * seq] for b in range(batch)])
    p_conv = tail(xbc, keep)[None]
    s_conv = jnp.concatenate([state_conv[0].astype(F32), xbc[t_p:].reshape(dec_batch, dec_seq, cw)], axis=1)[:, -keep:][None]
    wk_p = min(WINDOW, seq)
    p_wk = tail(k1, wk_p).reshape(1, batch, wk_p, SWA_KV_HEADS, SWA_HEAD_DIM)
    p_wv = tail(v1, wk_p).reshape(1, batch, wk_p, SWA_KV_HEADS, SWA_HEAD_DIM)
    k1s = k1[t_p:].reshape(dec_batch, dec_seq, SWA_KV_HEADS, SWA_HEAD_DIM)
    v1s = v1[t_p:].reshape(dec_batch, dec_seq, SWA_KV_HEADS, SWA_HEAD_DIM)
    wb = cache_win_k.shape[2]
    s_wk = jnp.concatenate([cache_win_k[0].astype(F32), k1s], axis=1)[:, -wb:][None]
    s_wv = jnp.concatenate([cache_win_v[0].astype(F32), v1s], axis=1)[:, -wb:][None]
    return (y_prompt.reshape(batch, seq, d), y_sample.reshape(dec_batch, dec_seq, d),
            p_k, p_v, s_k, s_v, p_conv, s_conv, p_ssm[None], s_ssm[None], p_wk, p_wv, s_wk, s_wv)
```

```python
import functools
import math

import jax
import jax.numpy as jnp
from jax import lax
from jax.experimental import pallas as pl
from jax.experimental.pallas import tpu as pltpu

F32 = jnp.float32
BF16 = jnp.bfloat16

SSD_HEAD_DIM = 64
SSD_GROUPS = 2
SSD_STATE = 128
SSD_CONV = 4
SSD_CHUNK = 128
DIFF_HEAD_DIM = 64
DIFF_KV_HEADS = 4
SWA_HEAD_DIM = 64
SWA_KV_HEADS = 4
WINDOW = 128
ROPE_THETA = 500000.0
ROPE_ROT = 16
EPS = 1e-6
LOG2E = math.log2(math.e)
MAX_LOG2_JUMP = 64.0

LANES = 128
HEAD = 64
VMEM_LIMIT = 56 * 1024 * 1024


def _dot(a, b):
    return jnp.dot(a, b, preferred_element_type=F32)


def _dot_nt(a, b):
    return lax.dot_general(a, b, (((1,), (1,)), ((), ())), preferred_element_type=F32)


def _split2_dot(v, m):
    hi = v.astype(BF16)
    lo = (v - hi.astype(F32)).astype(BF16)
    return _dot(hi, m) + _dot(lo, m)


def _split3_dot(m, v):
    hi = v.astype(BF16)
    r1 = v - hi.astype(F32)
    mid = r1.astype(BF16)
    lo = (r1 - mid.astype(F32)).astype(BF16)
    return _dot(m, hi) + _dot(m, mid) + _dot(m, lo)


def _silu(x):
    return x * (1.0 / (1.0 + jnp.exp(-x)))


def _softplus(x):
    return jnp.maximum(x, 0.0) + jnp.log(1.0 + jnp.exp(-jnp.abs(x)))


def _lane_half(shape):
    return (lax.broadcasted_iota(jnp.int32, shape, len(shape) - 1) % LANES) // HEAD


def _params(sem, vmem=VMEM_LIMIT):
    return pltpu.CompilerParams(dimension_semantics=sem, vmem_limit_bytes=vmem)


def _const_spec(shape):
    nd = len(shape)
    return pl.BlockSpec(shape, lambda *_: (0,) * nd)


def _row_specs(src, tm, n_p):
    if isinstance(src, tuple):
        p, s = src
        wp = p.shape[1]
        return [pl.BlockSpec((tm, wp), lambda i, *_: (jnp.minimum(i, n_p - 1), 0)),
                pl.BlockSpec((tm, wp), lambda i, *_: (jnp.maximum(i - n_p, 0), 0))], [p, s]
    return [pl.BlockSpec((tm, src.shape[1]), lambda i, *_: (i, 0))], [src]


def _row_load(refs, i, n_p, dtype):
    if len(refs) == 2:
        return jnp.where(i < n_p, refs[0][...].astype(dtype), refs[1][...].astype(dtype))
    return refs[0][...].astype(dtype)


def _rope_tables(seq, past_len, dec_seq, tm):
    half = ROPE_ROT // 2
    inv_freq = ROPE_THETA ** (-jnp.arange(half, dtype=F32) * (2.0 / ROPE_ROT))
    pos = jnp.concatenate([jnp.arange(seq, dtype=jnp.int32),
                           past_len + (jnp.arange(tm, dtype=jnp.int32) % dec_seq)]).astype(F32)
    ang = pos[:, None] * inv_freq[None, :]
    cos, sin = jnp.cos(ang), jnp.sin(ang)
    d = jnp.arange(LANES) % HEAD
    cos_l = jnp.where(d < ROPE_ROT, cos[:, d % half], 1.0)
    sin_a = jnp.where(d < half, -sin[:, d % half], 0.0)
    sin_b = jnp.where((d >= half) & (d < ROPE_ROT), sin[:, d % half], 0.0)
    return cos_l.astype(F32), sin_a.astype(F32), sin_b.astype(F32)


def _head_norm_rope(x, gain, bd, cos_l, sin_a, sin_b, scale):
    ms = _dot((x * x).astype(BF16), bd)
    y = x * lax.rsqrt(ms + EPS) * gain
    half = ROPE_ROT // 2
    y = y * cos_l + pltpu.roll(y, LANES - half, 1) * sin_a + pltpu.roll(y, half, 1) * sin_b
    if scale != 1.0:
        y = y * scale
    return y


def _norm_proj_kernel(*refs, n_src, n_p, segs, n_out):
    src = refs[:n_src]
    g_ref, cos_ref, sa_ref, sb_ref, bd_ref = refs[n_src:n_src + 5]
    pos = n_src + 5
    w_refs = refs[pos:pos + len(segs)]
    pos += len(segs)
    n_gain = sum(1 for s in segs if s["kind"] == "qk")
    gain_refs = refs[pos:pos + n_gain]
    pos += n_gain
    out_refs = refs[pos:pos + n_out]

    i = pl.program_id(0)
    x = _row_load(src, i, n_p, F32)
    xn = (x * lax.rsqrt(jnp.mean(x * x, axis=-1, keepdims=True) + EPS) * g_ref[...]).astype(BF16)

    oi = 0
    gi = 0
    tm = x.shape[0]
    head_rows = []
    for s, w_ref in zip(segs, w_refs):
        width = s["width"]
        n_refs = sum(2 if spec.get("rows") else 1 for spec in s["outs"])
        outs = out_refs[oi:oi + n_refs]
        oi += n_refs
        if s["kind"] == "qk":
            gain = gain_refs[gi][...]
            gi += 1
        step = min(width, 512)
        lane_wise = s["kind"] != "plain" or any(spec.get("rows") or spec.get("dup") for spec in s["outs"])
        for c0 in range(0, width, step):
            acc = _dot(xn, w_ref[:, c0:c0 + step])
            if not lane_wise:
                for o_ref, spec in zip(outs, s["outs"]):
                    o_ref[:, c0:c0 + step] = acc.astype(spec["dtype"])
                continue
            for l0 in range(0, step, LANES):
                y = acc[:, l0:l0 + LANES]
                if s["kind"] == "qk":
                    y = _head_norm_rope(y, gain, bd_ref[...], cos_ref[...], sa_ref[...], sb_ref[...], s["scale"])
                col = c0 + l0
                ri = 0
                for spec in s["outs"]:
                    o_ref = outs[ri]
                    if spec.get("rows"):
                        head_rows.append((outs[ri], outs[ri + 1], pl.ds(col // LANES, tm, stride=spec["rows"]), y))
                        ri += 2
                        continue
                    ri += 1
                    if spec.get("dup"):
                        sw = pltpu.roll(y, HEAD, 1)
                        first = _lane_half(y.shape) == 0
                        o_ref[:, 2 * col:2 * col + LANES] = jnp.where(first, y, sw).astype(spec["dtype"])
                        o_ref[:, 2 * col + LANES:2 * col + 2 * LANES] = jnp.where(first, sw, y).astype(spec["dtype"])
                    else:
                        o_ref[:, col:col + LANES] = y.astype(spec["dtype"])

    if head_rows:
        @pl.when(i < n_p)
        def _():
            for op_ref, _, tgt, y in head_rows:
                op_ref[tgt, :] = y.astype(op_ref.dtype)

        @pl.when(i >= n_p)
        def _():
            for _, os_ref, tgt, y in head_rows:
                os_ref[tgt, :] = y.astype(os_ref.dtype)


def _norm_proj(src, gain, segs, rope, bd, tm, n_p, n_t):
    d_model = gain.shape[-1]
    row_specs, row_args = _row_specs(src, tm, n_p)
    cos_l, sin_a, sin_b = rope
    n_pb = (cos_l.shape[0] - tm) // tm
    rope_spec = pl.BlockSpec((tm, LANES), lambda i: (jnp.where(i < n_p, i % n_pb, n_pb), 0))
    in_specs = row_specs + [_const_spec((1, d_model)), rope_spec, rope_spec, rope_spec, _const_spec((LANES, LANES))]
    args = row_args + [gain.reshape(1, d_model), cos_l, sin_a, sin_b, bd]
    for s in segs:
        in_specs.append(_const_spec(s["w"].shape))
        args.append(s["w"])
    for s in segs:
        if s["kind"] == "qk":
            in_specs.append(_const_spec((1, LANES)))
            args.append(jnp.tile(s["gain"].astype(F32), LANES // HEAD).reshape(1, LANES))
    out_shapes, out_specs = [], []
    for s in segs:
        for spec in s["outs"]:
            if spec.get("rows"):
                nr = spec["rows"]
                assert s["width"] == nr * LANES
                out_shapes += [jax.ShapeDtypeStruct((n_p * tm * nr, LANES), spec["dtype"]),
                               jax.ShapeDtypeStruct(((n_t - n_p) * tm * nr, LANES), spec["dtype"])]
                out_specs += [pl.BlockSpec((tm * nr, LANES), lambda i: (jnp.minimum(i, n_p - 1), 0)),
                              pl.BlockSpec((tm * nr, LANES), lambda i: (jnp.maximum(i - n_p, 0), 0))]
                continue
            w = s["width"] * (2 if spec.get("dup") else 1)
            out_shapes.append(jax.ShapeDtypeStruct((n_t * tm, w), spec["dtype"]))
            out_specs.append(pl.BlockSpec((tm, w), lambda i: (i, 0)))
    kern = functools.partial(
        _norm_proj_kernel, n_src=len(row_args), n_p=n_p,
        segs=[{k: v for k, v in s.items() if k not in ("w", "gain")} for s in segs], n_out=len(out_shapes))
    return pl.pallas_call(
        kern, grid=(n_t,), in_specs=in_specs, out_specs=out_specs, out_shape=out_shapes,
        compiler_params=_params(("arbitrary",)), name="norm_proj")(*args)


def _conv_rolls(u, prevpad, w4, bias, l_sub):
    rows = lax.broadcasted_iota(jnp.int32, u.shape, 0) % l_sub
    n = u.shape[0]
    y = bias + w4[SSD_CONV - 1:SSD_CONV, :] * u
    for k in range(SSD_CONV - 1):
        j = SSD_CONV - 1 - k
        cur = pltpu.roll(u, j, 0)
        prv = pltpu.roll(prevpad, n + j - 8, 0)
        y = y + w4[k:k + 1, :] * jnp.where(rows >= j, cur, prv)
    return y


def _ssd_block(xs_c, bm, cm, dt_raw, z, dtb, a_log, d_x, nw, eexp, hprev_fn, hstore_fn, l_sub):
    n = xs_c.shape[0]
    n_seq = n // l_sub
    hpg = xs_c.shape[1] // SSD_HEAD_DIM
    row = lax.broadcasted_iota(jnp.int32, (n, n), 0)
    col = lax.broadcasted_iota(jnp.int32, (n, n), 1)
    same = (row // l_sub) == (col // l_sub)
    tri = same & (col <= row)

    dt = _softplus(dt_raw + dtb)
    a = dt * (-jnp.exp(a_log))
    a_cum = _split3_dot(jnp.where(tri, 1.0, 0.0).astype(BF16), a)
    a_tot = _split3_dot(jnp.where(same, 1.0, 0.0).astype(BF16), a)
    a_cum_t = a_cum.T
    e_tot_t = jnp.exp(a_tot.T)

    xdt = xs_c * _split2_dot(dt, eexp)
    bm_b = bm.astype(BF16)
    cm_b = cm.astype(BF16)
    g_cb = _dot_nt(cm_b, bm_b)

    half = _lane_half((n, LANES))
    y_parts = []
    for pr in range(hpg // 2):
        ms = []
        for hh in range(2):
            h = 2 * pr + hh
            diff = a_cum[:, h:h + 1] - a_cum_t[h:h + 1, :]
            ms.append((g_cb * jnp.exp(jnp.where(tri, diff, -jnp.inf))).astype(BF16))
        xp = xdt[:, pr * LANES:(pr + 1) * LANES]
        xcat = jnp.concatenate([jnp.where(half == 0, xp, 0.0), jnp.where(half == 1, xp, 0.0)], axis=0).astype(BF16)
        y_parts.append(_dot(jnp.concatenate(ms, axis=1), xcat))
    y = jnp.concatenate(y_parts, axis=1)

    xw_t = (xdt * _split2_dot(jnp.exp(a_tot - a_cum), eexp)).T.astype(BF16)
    seq_row = lax.broadcasted_iota(jnp.int32, (n, 1), 0) // l_sub
    y_off = jnp.zeros_like(y)
    for b in range(n_seq):
        h_b = hprev_fn(b)
        r_b = _dot_nt(cm_b, h_b.reshape(hpg * SSD_HEAD_DIM, SSD_STATE).astype(BF16))
        bm_sel = bm_b if n_seq == 1 else jnp.where(seq_row == b, bm, 0.0).astype(BF16)
        s_b = _dot(xw_t, bm_sel)
        y_off = r_b if n_seq == 1 else jnp.where(seq_row == b, r_b, y_off)
        t0 = b * l_sub
        for h in range(hpg):
            hstore_fn(b, h, e_tot_t[h:h + 1, t0:t0 + 1] * h_b[h] + s_b[h * SSD_HEAD_DIM:(h + 1) * SSD_HEAD_DIM, :])
    y = y + y_off * _split2_dot(jnp.exp(a_cum), eexp) + xs_c * d_x

    gated = y * _silu(z.astype(F32))
    return (gated * lax.rsqrt(jnp.mean(gated * gated, axis=-1, keepdims=True) + EPS) * nw).astype(BF16)


def _ssd_prompt_kernel(xs_ref, b_ref, c_ref, hxs_ref, hb_ref, hc_ref, dt_ref, z_ref,
                       wxs_ref, wb_ref, wc_ref, bxs_ref, bb_ref, bc_ref,
                       dtb_ref, alog_ref, dx_ref, nw_ref, eexp_ref, y_ref, st_ref):
    c = pl.program_id(2)

    @pl.when(c == 0)
    def _():
        st_ref[...] = jnp.zeros_like(st_ref)

    keep = jnp.where(c > 0, 1.0, 0.0)

    def conv(u_ref, h_ref, w_ref, bias_ref):
        u = u_ref[...]
        prevpad = jnp.concatenate([h_ref[...] * keep, jnp.zeros((u.shape[0] - 8, u.shape[1]), F32)], axis=0)
        return _silu(_conv_rolls(u, prevpad, w_ref[...], bias_ref[...], u.shape[0]))

    xs_c = conv(xs_ref, hxs_ref, wxs_ref, bxs_ref)
    bm = conv(b_ref, hb_ref, wb_ref, bb_ref)
    cm = conv(c_ref, hc_ref, wc_ref, bc_ref)

    def hprev(b):
        return st_ref[0]

    def hstore(b, h, val):
        st_ref[0, h] = val

    y_ref[...] = _ssd_block(xs_c, bm, cm, dt_ref[...], z_ref[...], dtb_ref[...], alog_ref[...], dx_ref[...],
                            nw_ref[...], eexp_ref[...], hprev, hstore, xs_c.shape[0])


def _ssd_sample_kernel(xs_ref, b_ref, c_ref, pxs_ref, pb_ref, pc_ref, dt_ref, z_ref,
                       wxs_ref, wb_ref, wc_ref, bxs_ref, bb_ref, bc_ref,
                       dtb_ref, alog_ref, dx_ref, nw_ref, eexp_ref, st_in_ref, y_ref, st_ref, *, l_sub):
    def conv(u_ref, p_ref, w_ref, bias_ref):
        return _silu(_conv_rolls(u_ref[...], p_ref[...], w_ref[...], bias_ref[...], l_sub))

    xs_c = conv(xs_ref, pxs_ref, wxs_ref, bxs_ref)
    bm = conv(b_ref, pb_ref, wb_ref, bb_ref)
    cm = conv(c_ref, pc_ref, wc_ref, bc_ref)

    def hprev(b):
        return st_in_ref[b]

    def hstore(b, h, val):
        st_ref[b, h] = val

    y_ref[...] = _ssd_block(xs_c, bm, cm, dt_ref[...], z_ref[...], dtb_ref[...], alog_ref[...], dx_ref[...],
                            nw_ref[...], eexp_ref[...], hprev, hstore, l_sub)


def _ssd_common_specs(xw, gw, sw, row_of, ng):
    nb = xw // LANES
    def rs(w, colf):
        return pl.BlockSpec((SSD_CHUNK, w), lambda *g: (row_of(*g), colf(g[1])))
    return dict(
        xs=rs(gw, lambda g: g), b=rs(sw, lambda g: nb + g), c=rs(sw, lambda g: nb + ng + g),
        wxs=pl.BlockSpec((SSD_CONV, gw), lambda *g: (0, g[1])),
        wb=pl.BlockSpec((SSD_CONV, sw), lambda *g: (0, nb + g[1])),
        wc=pl.BlockSpec((SSD_CONV, sw), lambda *g: (0, nb + ng + g[1])),
        bxs=pl.BlockSpec((1, gw), lambda *g: (0, g[1])),
        bb=pl.BlockSpec((1, sw), lambda *g: (0, nb + g[1])),
        bc=pl.BlockSpec((1, sw), lambda *g: (0, nb + ng + g[1])),
        lane=pl.BlockSpec((1, LANES), lambda *g: (0, g[1])),
        grp=pl.BlockSpec((1, gw), lambda *g: (0, g[1])),
        eexp=_const_spec((LANES, gw)),
    )


def _ssd_prompt(xbc, dt, z, conv_w, conv_b, dtb, alog, d_x, nw, eexp, batch, seq):
    ng = SSD_GROUPS
    xw = z.shape[1]
    gw, sw = xw // ng, SSD_STATE
    hpg = gw // SSD_HEAD_DIM
    nc = seq // SSD_CHUNK
    row_of = lambda b, g, c: b * nc + c
    sp = _ssd_common_specs(xw, gw, sw, row_of, ng)
    nb = xw // LANES
    sub = SSD_CHUNK // 8

    def halo(w, colf):
        return pl.BlockSpec((8, w), lambda b, g, c: (jnp.maximum((b * nc + c) * sub - 1, 0), colf(g)))

    in_specs = [sp["xs"], sp["b"], sp["c"],
                halo(gw, lambda g: g), halo(sw, lambda g: nb + g), halo(sw, lambda g: nb + ng + g),
                pl.BlockSpec((SSD_CHUNK, LANES), lambda b, g, c: (row_of(b, g, c), g)),
                pl.BlockSpec((SSD_CHUNK, gw), lambda b, g, c: (row_of(b, g, c), g)),
                sp["wxs"], sp["wb"], sp["wc"], sp["bxs"], sp["bb"], sp["bc"],
                sp["lane"], sp["lane"], sp["grp"], sp["grp"], sp["eexp"]]
    out_specs = [pl.BlockSpec((SSD_CHUNK, gw), lambda b, g, c: (row_of(b, g, c), g)),
                 pl.BlockSpec((1, hpg, SSD_HEAD_DIM, SSD_STATE), lambda b, g, c: (b, g, 0, 0))]
    out_shape = [jax.ShapeDtypeStruct((batch * seq, xw), BF16),
                 jax.ShapeDtypeStruct((batch, ng * hpg, SSD_HEAD_DIM, SSD_STATE), F32)]
    return pl.pallas_call(
        _ssd_prompt_kernel, grid=(batch, ng, nc), in_specs=in_specs, out_specs=out_specs, out_shape=out_shape,
        compiler_params=_params(("arbitrary", "arbitrary", "arbitrary")), name="ssd_prompt")(
            xbc, xbc, xbc, xbc, xbc, xbc, dt, z, conv_w, conv_w, conv_w, conv_b, conv_b, conv_b,
            dtb, alog, d_x, nw, eexp)


def _ssd_sample(xbc, prev8, dt, z, conv_w, conv_b, dtb, alog, d_x, nw, eexp, state, t_p, dec_batch, dec_seq):
    ng = SSD_GROUPS
    xw = z.shape[1]
    gw, sw = xw // ng, SSD_STATE
    hpg = gw // SSD_HEAD_DIM
    n_seq = SSD_CHUNK // dec_seq
    nblk = dec_batch // n_seq
    rb0 = t_p // SSD_CHUNK
    row_of = lambda i, g: rb0 + i
    sp = _ssd_common_specs(xw, gw, sw, row_of, ng)
    nb = xw // LANES

    def prev(w, colf):
        return pl.BlockSpec((SSD_CHUNK, w), lambda i, g: (i, colf(g)))

    st_spec = pl.BlockSpec((n_seq, hpg, SSD_HEAD_DIM, SSD_STATE), lambda i, g: (i, g, 0, 0))
    in_specs = [sp["xs"], sp["b"], sp["c"],
                prev(gw, lambda g: g), prev(sw, lambda g: nb + g), prev(sw, lambda g: nb + ng + g),
                pl.BlockSpec((SSD_CHUNK, LANES), lambda i, g: (rb0 + i, g)),
                pl.BlockSpec((SSD_CHUNK, gw), lambda i, g: (rb0 + i, g)),
                sp["wxs"], sp["wb"], sp["wc"], sp["bxs"], sp["bb"], sp["bc"],
                sp["lane"], sp["lane"], sp["grp"], sp["grp"], sp["eexp"], st_spec]
    out_specs = [pl.BlockSpec((SSD_CHUNK, gw), lambda i, g: (i, g)), st_spec]
    out_shape = [jax.ShapeDtypeStruct((dec_batch * dec_seq, xw), BF16),
                 jax.ShapeDtypeStruct(state.shape, F32)]
    return pl.pallas_call(
        functools.partial(_ssd_sample_kernel, l_sub=dec_seq), grid=(nblk, ng), in_specs=in_specs,
        out_specs=out_specs, out_shape=out_shape,
        compiler_params=_params(("arbitrary", "arbitrary")), name="ssd_sample")(
            xbc, xbc, xbc, prev8, prev8, prev8, dt, z, conv_w, conv_w, conv_w, conv_b, conv_b, conv_b,
            dtb, alog, d_x, nw, eexp, state)


def _lambda(lq1, lk1, lq2, lk2, lam_init):
    s1 = jnp.sum(lq1[...] * lk1[...], axis=-1, keepdims=True)
    s2 = jnp.sum(lq2[...] * lk2[...], axis=-1, keepdims=True)
    return jnp.exp(s1) - jnp.exp(s2) + lam_init


def _subln(att, w, lam_init):
    return att * lax.rsqrt(jnp.mean(att * att, axis=-1, keepdims=True) + EPS) * w * (1.0 - lam_init)


def _lane_tile(x, n):
    return x if n == 1 else jnp.concatenate([x] * n, axis=1)


def _diff_prompt_kernel(q_ref, k_ref, v_ref, lq1, lk1, lq2, lk2, sub_ref, o_ref,
                        lhs_ref, vext_ref, s_ref, m_ref, acc_ref, *, tq, tk, lam_init):
    qb = pl.program_id(2)
    n_g = q_ref.shape[1] // LANES
    rows = 2 * n_g * tq
    nl = tk // LANES

    @pl.when(qb == 0)
    def _():
        vext_ref[:, :LANES] = v_ref[...]
        vext_ref[:, LANES:] = jnp.ones((v_ref.shape[0], LANES), vext_ref.dtype)

    half = _lane_half((tq, LANES))
    for j in range(2):
        for g in range(n_g):
            qg = q_ref[:, g * LANES:(g + 1) * LANES]
            r0 = (j * n_g + g) * tq
            lhs_ref[r0:r0 + tq, :] = jnp.where(half == j, qg, jnp.zeros_like(qg))
    m_ref[...] = jnp.full(m_ref.shape, -jnp.inf, F32)
    acc_ref[...] = jnp.zeros(acc_ref.shape, F32)
    n_t = (qb * tq) // tk + 1

    def scores(t, masked):
        k = k_ref[pl.ds(pl.multiple_of(t * tk, tk), tk), :]
        s = _dot_nt(lhs_ref[...], k)
        if masked:
            qpos = qb * tq + lax.broadcasted_iota(jnp.int32, (rows, tk), 0) % tq
            kpos = t * tk + lax.broadcasted_iota(jnp.int32, (rows, tk), 1)
            s = jnp.where(kpos <= qpos, s, -jnp.inf)
        s_ref[t] = s
        mt = s[:, :LANES]
        for c in range(1, nl):
            mt = jnp.maximum(mt, s[:, c * LANES:(c + 1) * LANES])
        m_ref[...] = jnp.maximum(m_ref[...], mt)

    lax.fori_loop(0, n_t - 1, lambda t, c: (scores(t, False), c)[1], 0)
    scores(n_t - 1, True)
    m_ref[...] = jnp.broadcast_to(jnp.max(m_ref[...], axis=-1, keepdims=True), m_ref.shape)

    def attend(t, c):
        p = jnp.exp2(s_ref[t] - _lane_tile(m_ref[...], nl)).astype(BF16)
        acc_ref[...] += _dot(p, vext_ref[pl.ds(pl.multiple_of(t * tk, tk), tk), :])
        return c

    lax.fori_loop(0, n_t, attend, 0)

    lam = _lambda(lq1, lk1, lq2, lk2, lam_init)
    acc = acc_ref[...]
    o = acc[:, :LANES] / acc[:, LANES:]
    att = o[:rows // 2] - lam * o[rows // 2:]
    res = _subln(att, sub_ref[...], lam_init)
    for g in range(n_g):
        o_ref[:, g * LANES:(g + 1) * LANES] = res[g * tq:(g + 1) * tq].astype(o_ref.dtype)


def _diff_prompt(q, kb, vb, lam_vecs, subln, batch, seq, lam_init):
    t_p = batch * seq
    n_kvh = DIFF_KV_HEADS
    qw = q.shape[1] // n_kvh
    kw = kb.shape[1] // n_kvh
    tq = min(512, seq)
    tk = min(512, seq)
    assert seq % tk == 0 and tk % tq == 0 and kw == LANES
    nq = seq // tq
    rows = 2 * (qw // LANES) * tq
    vec = _const_spec((1, DIFF_HEAD_DIM))
    in_specs = [pl.BlockSpec((tq, qw), lambda b, h, i: (b * nq + i, h)),
                pl.BlockSpec((seq, kw), lambda b, h, i: (b, h)),
                pl.BlockSpec((seq, kw), lambda b, h, i: (b, h)),
                vec, vec, vec, vec, _const_spec((1, LANES))]
    return pl.pallas_call(
        functools.partial(_diff_prompt_kernel, tq=tq, tk=tk, lam_init=lam_init),
        grid=(batch, n_kvh, nq), in_specs=in_specs,
        out_specs=pl.BlockSpec((tq, qw), lambda b, h, i: (b * nq + i, h)),
        out_shape=jax.ShapeDtypeStruct((t_p, q.shape[1]), BF16),
        scratch_shapes=[pltpu.VMEM((rows, LANES), BF16), pltpu.VMEM((seq, 2 * LANES), BF16),
                        pltpu.VMEM((seq // tk, rows, tk), F32), pltpu.VMEM((rows, LANES), F32),
                        pltpu.VMEM((rows, 2 * LANES), F32)],
        compiler_params=_params(("arbitrary", "arbitrary", "arbitrary")), name="diff_attn_prompt")(
            q, kb, vb, *lam_vecs, subln)


def _diff_paged_kernel(pt_ref, q_ref, kn_ref, vn_ref, ck_hbm, cv_hbm, lq1, lk1, lq2, lk2, sub_ref, o_ref,
                       kbuf, vbuf, sem, qst_ref, m_ref, acc_ref, *, n_pg, dec_seq, lam_init):
    p = pl.program_id(1)
    n_steps = pl.num_programs(1)
    step = pl.program_id(0) * n_steps + p
    last_step = pl.num_programs(0) * n_steps - 1
    slot = step % 2
    n_kvh = DIFF_KV_HEADS
    n_g = q_ref.shape[1] // (n_kvh * LANES)
    rph = n_g * 2 * dec_seq
    rows = n_kvh * rph

    def page_copies(s, sl, lookup):
        bb, pp = s // n_steps, s % n_steps
        out = []
        for i in range(n_pg):
            pg = pt_ref[bb, pp * n_pg + i] if lookup else 0
            out.append(pltpu.make_async_copy(ck_hbm.at[pg], kbuf.at[sl, i], sem.at[sl]))
            out.append(pltpu.make_async_copy(cv_hbm.at[pg], vbuf.at[sl, i], sem.at[sl]))
        return out

    @pl.when(step == 0)
    def _():
        for c in page_copies(step, slot, True):
            c.start()

    @pl.when(step < last_step)
    def _():
        for c in page_copies(step + 1, 1 - slot, True):
            c.start()

    for c in page_copies(step, slot, False):
        c.wait()
    k_pages = [kbuf.at[slot, i] for i in range(n_pg)]
    v_pages = [vbuf.at[slot, i] for i in range(n_pg)]

    own_lane = (lax.broadcasted_iota(jnp.int32, (rows, LANES), 1) % n_kvh
                == lax.broadcasted_iota(jnp.int32, (rows, LANES), 0) // rph)
    lane_bias = jnp.where(own_lane, 0.0, -jnp.inf)

    def scores(qst, k):
        return _dot_nt(qst, k.astype(BF16))

    def block_max(s, mt=None):
        for c in range(s.shape[1] // LANES):
            blk = s[:, c * LANES:(c + 1) * LANES]
            mt = blk if mt is None else jnp.maximum(mt, blk)
        return mt

    def probs_times_v(s, shift, v):
        pr = jnp.exp2(s + _lane_tile(shift, s.shape[1] // LANES)).astype(BF16)
        return _dot(pr, jnp.concatenate([v.astype(BF16), jnp.ones(v.shape, BF16)], axis=1))

    def exact_update(s, bias, v, lane_ok):
        m_old = m_ref[...]
        m_new = jnp.maximum(m_old, jnp.max(jnp.where(lane_ok, block_max(s), -jnp.inf), axis=-1, keepdims=True))
        acc_ref[...] = acc_ref[...] * _lane_tile(jnp.exp2(m_old - m_new), 2) + probs_times_v(s, bias - m_new, v)
        m_ref[...] = m_new

    @pl.when(p == 0)
    def _():
        half = _lane_half((dec_seq, LANES))
        parts = []
        for h in range(n_kvh):
            for g in range(n_g):
                qg = q_ref[:, (h * n_g + g) * LANES:(h * n_g + g + 1) * LANES]
                for j in range(2):
                    parts.append(jnp.where(half == j, qg, 0.0))
        q0 = jnp.concatenate(parts, axis=0).astype(BF16)
        qst_ref[...] = q0
        m0 = jnp.max(jnp.where(own_lane, block_max(scores(q0, k_pages[0][...])), -jnp.inf), axis=-1, keepdims=True)
        m_ref[...] = jnp.broadcast_to(m0, m_ref.shape)
        acc_ref[...] = jnp.zeros(acc_ref.shape, F32)

    qst = qst_ref[...]
    m_old = m_ref[...]
    shift = lane_bias - m_old
    mt, pv = None, None
    for i in range(n_pg):
        s = scores(qst, k_pages[i][...])
        mt = block_max(s, mt)
        t = probs_times_v(s, shift, v_pages[i][...])
        pv = t if pv is None else pv + t
    m_step = jnp.max(jnp.where(own_lane, mt, -jnp.inf), axis=-1, keepdims=True)
    in_range = jnp.max(m_step - m_old) <= MAX_LOG2_JUMP

    @pl.when(in_range)
    def _():
        m_new = jnp.maximum(m_old, m_step)
        acc_ref[...] = (acc_ref[...] + pv) * _lane_tile(jnp.exp2(m_old - m_new), 2)
        m_ref[...] = m_new

    @pl.when(jnp.logical_not(in_range))
    def _():
        for i in range(n_pg):
            exact_update(scores(qst, k_pages[i][...]), lane_bias, v_pages[i][...], own_lane)

    @pl.when(p == n_steps - 1)
    def _():
        lam = _lambda(lq1, lk1, lq2, lk2, lam_init)
        pad = jnp.zeros((LANES - n_kvh * dec_seq, LANES), F32)
        kn = jnp.concatenate([kn_ref[...], pad], axis=0)
        vn = jnp.concatenate([vn_ref[...], pad], axis=0)
        ri = lax.broadcasted_iota(jnp.int32, (rows, LANES), 0)
        ci = lax.broadcasted_iota(jnp.int32, (rows, LANES), 1)
        ok = (ci % n_kvh == ri // rph) & (ci // n_kvh <= ri % dec_seq)
        exact_update(scores(qst, kn), jnp.where(ok, 0.0, -jnp.inf), vn, ok)
        acc = acc_ref[...]
        o = acc[:, :LANES] / acc[:, LANES:]
        for h in range(n_kvh):
            for g in range(n_g):
                r0 = h * rph + g * 2 * dec_seq
                att = o[r0:r0 + dec_seq] - lam * o[r0 + dec_seq:r0 + 2 * dec_seq]
                c0 = (h * n_g + g) * LANES
                o_ref[:, c0:c0 + LANES] = _subln(att, sub_ref[...], lam_init).astype(o_ref.dtype)


def _diff_paged(q_s, k_new, v_new, cache_k, cache_v, page_table, lam_vecs, subln, dec_batch, dec_seq, lam_init):
    n_pages = page_table.shape[1]
    n_pg = math.gcd(n_pages, 8)
    n_kvh = DIFF_KV_HEADS
    page_rows = cache_k.shape[1]
    n_g = q_s.shape[1] // (n_kvh * LANES)
    rows = n_kvh * n_g * 2 * dec_seq
    nr = n_kvh * dec_seq
    assert nr <= LANES and k_new.shape == (dec_batch * nr, LANES)
    vec = _const_spec((1, DIFF_HEAD_DIM))

    hbm = pl.BlockSpec(memory_space=pl.ANY)
    in_specs = [pl.BlockSpec((dec_seq, q_s.shape[1]), lambda b, p, pt: (b, 0)),
                pl.BlockSpec((nr, LANES), lambda b, p, pt: (b, 0)),
                pl.BlockSpec((nr, LANES), lambda b, p, pt: (b, 0)),
                hbm, hbm, vec, vec, vec, vec, _const_spec((1, LANES))]
    grid_spec = pltpu.PrefetchScalarGridSpec(
        num_scalar_prefetch=1, grid=(dec_batch, n_pages // n_pg), in_specs=in_specs,
        out_specs=pl.BlockSpec((dec_seq, q_s.shape[1]), lambda b, p, pt: (b, 0)),
        scratch_shapes=[pltpu.VMEM((2, n_pg, page_rows, LANES), cache_k.dtype),
                        pltpu.VMEM((2, n_pg, page_rows, LANES), cache_v.dtype),
                        pltpu.SemaphoreType.DMA((2,)),
                        pltpu.VMEM((rows, LANES), BF16), pltpu.VMEM((rows, LANES), F32),
                        pltpu.VMEM((rows, 2 * LANES), F32)])
    return pl.pallas_call(
        functools.partial(_diff_paged_kernel, n_pg=n_pg, dec_seq=dec_seq, lam_init=lam_init),
        grid_spec=grid_spec, out_shape=jax.ShapeDtypeStruct(q_s.shape, F32),
        compiler_params=_params(("arbitrary", "arbitrary")), name="diff_attn_paged")(
            page_table, q_s, k_new, v_new, cache_k, cache_v, *lam_vecs, subln)


def _sink_attend(lhs, blocks, sink):
    ss = [_dot_nt(lhs, k2) + bias for k2, _, bias in blocks]
    mt = None
    for s in ss:
        for c in range(s.shape[1] // LANES):
            blk = s[:, c * LANES:(c + 1) * LANES]
            mt = blk if mt is None else jnp.maximum(mt, blk)
    m = jnp.maximum(jnp.max(mt, axis=-1, keepdims=True), sink)
    acc = None
    for s, (_, v2, _) in zip(ss, blocks):
        p = jnp.exp2(s - _lane_tile(m, s.shape[1] // LANES)).astype(BF16)
        t = _dot(p, jnp.concatenate([v2, jnp.ones(v2.shape, BF16)], axis=1))
        acc = t if acc is None else acc + t
    return acc[:, :LANES] / (acc[:, LANES:] + jnp.exp2(sink - m))


def _swa_prompt_kernel(q_ref, kc_ref, kp_ref, vc_ref, vp_ref, sink_ref, o_ref):
    n = pl.program_id(1)
    w = q_ref.shape[0]
    n_kvh = SWA_KV_HEADS
    n_g = q_ref.shape[1] // (n_kvh * HEAD)
    half = _lane_half((w, LANES))
    qi = lax.broadcasted_iota(jnp.int32, (w, 2 * w), 0)
    ci = lax.broadcasted_iota(jnp.int32, (w, 2 * w), 1)
    ok = (ci > qi) & (ci <= qi + w) & ((ci >= w) | (n > 0))
    bias1 = jnp.where(ok, 0.0, -jnp.inf)
    bias = jnp.concatenate([bias1] * n_g, axis=0)
    sink_all = sink_ref[...] * LOG2E
    for h in range(n_kvh):
        cols = slice(h * LANES, (h + 1) * LANES)
        k2 = jnp.concatenate([kp_ref[:, cols], kc_ref[:, cols]], axis=0)
        v2 = jnp.concatenate([vp_ref[:, cols], vc_ref[:, cols]], axis=0)
        parts, sinks = [], []
        for g in range(n_g):
            hd = h * n_g + g
            qg = q_ref[:, (hd // 2) * LANES:(hd // 2 + 1) * LANES]
            parts.append(jnp.where(half == hd % 2, qg, jnp.zeros_like(qg)))
            sinks.append(jnp.broadcast_to(sink_all[0:1, hd:hd + 1], (w, LANES)))
        o2 = _sink_attend(jnp.concatenate(parts, axis=0), [(k2, v2, bias)], jnp.concatenate(sinks, axis=0))
        for a in range(n_g // 2):
            hd = h * n_g + 2 * a
            pair = jnp.where(half == 0, o2[2 * a * w:(2 * a + 1) * w], o2[(2 * a + 1) * w:(2 * a + 2) * w])
            o_ref[:, (hd // 2) * LANES:(hd // 2 + 1) * LANES] = pair.astype(o_ref.dtype)


def _swa_prompt(q, k2, v2, sinks, batch, seq):
    w = WINDOW
    nb = seq // w
    kw = k2.shape[1]
    cur = pl.BlockSpec((w, kw), lambda b, n: (b * nb + n, 0))
    prv = pl.BlockSpec((w, kw), lambda b, n: (jnp.maximum(b * nb + n - 1, 0), 0))
    in_specs = [pl.BlockSpec((w, q.shape[1]), lambda b, n: (b * nb + n, 0)), cur, prv, cur, prv,
                _const_spec(sinks.shape)]
    return pl.pallas_call(
        _swa_prompt_kernel, grid=(batch, nb), in_specs=in_specs,
        out_specs=pl.BlockSpec((w, q.shape[1]), lambda b, n: (b * nb + n, 0)),
        out_shape=jax.ShapeDtypeStruct((batch * seq, q.shape[1]), BF16),
        compiler_params=_params(("arbitrary", "arbitrary")), name="swa_prompt")(q, k2, k2, v2, v2, sinks)


def _swa_sample_kernel(q_ref, wk_ref, wv_ref, kn_ref, vn_ref, sink_ref, o_ref, *, dec_seq):
    nbb, wb, kw = wk_ref.shape
    n_kvh = SWA_KV_HEADS
    n_h = q_ref.shape[1] // HEAD
    n_g = n_h // n_kvh
    rows = n_h * dec_seq
    nv = kw // LANES
    half8 = _lane_half((dec_seq, LANES))
    qi = lax.broadcasted_iota(jnp.int32, (rows, wb), 0) % dec_seq
    ci = lax.broadcasted_iota(jnp.int32, (rows, wb), 1)
    bias_c = jnp.where(ci > qi, 0.0, -jnp.inf)
    bias_n = jnp.where(ci <= qi, 0.0, -jnp.inf)
    pad = jnp.zeros((wb - dec_seq, kw), F32)
    zero8 = jnp.zeros((dec_seq, LANES), F32)
    sink_all = sink_ref[...] * LOG2E
    sink = jnp.concatenate([jnp.broadcast_to(sink_all[0:1, hd:hd + 1], (dec_seq, LANES)) for hd in range(n_h)], axis=0)

    for bb in range(nbb):
        r0 = bb * dec_seq
        q8 = q_ref[r0:r0 + dec_seq, :]
        pieces = []
        for hd in range(n_h):
            kvh = hd // n_g
            x = jnp.where(half8 == hd % 2, q8[:, (hd // 2) * LANES:(hd // 2 + 1) * LANES], 0.0)
            if hd % 2 != kvh % 2:
                x = pltpu.roll(x, HEAD, 1)
            pieces.append(jnp.concatenate([x if c == kvh // 2 else zero8 for c in range(nv)], axis=1))
        qbd = jnp.concatenate(pieces, axis=0).astype(BF16)
        kn = jnp.concatenate([kn_ref[r0:r0 + dec_seq, :], pad], axis=0).astype(BF16)
        vn = jnp.concatenate([vn_ref[r0:r0 + dec_seq, :], pad], axis=0).astype(BF16)
        s_c = _dot_nt(qbd, wk_ref[bb].astype(BF16)) + bias_c
        s_n = _dot_nt(qbd, kn) + bias_n
        m = jnp.maximum(jnp.max(jnp.maximum(s_c, s_n), axis=-1, keepdims=True), sink)
        p_c, p_n = jnp.exp2(s_c - m), jnp.exp2(s_n - m)
        den = jnp.sum(p_c + p_n, axis=-1, keepdims=True) + jnp.exp2(sink - m)
        o_all = (_dot(p_c.astype(BF16), wv_ref[bb].astype(BF16)) + _dot(p_n.astype(BF16), vn)) / _lane_tile(den, nv)
        for pr in range(n_h // 2):
            blks = []
            for e in range(2):
                hd = 2 * pr + e
                kvh = hd // n_g
                blk = o_all[hd * dec_seq:(hd + 1) * dec_seq, (kvh // 2) * LANES:(kvh // 2 + 1) * LANES]
                blks.append(pltpu.roll(blk, HEAD, 1) if kvh % 2 != e else blk)
            o_ref[r0:r0 + dec_seq, pr * LANES:(pr + 1) * LANES] = jnp.where(half8 == 0, blks[0], blks[1])


def _swa_sample(q_s, wk, wv, k_all, v_all, sinks, t_p, dec_batch, dec_seq):
    nbb = math.gcd(dec_batch, 8)
    rows = nbb * dec_seq
    rb0 = t_p // rows
    kw = k_all.shape[1]
    wb = wk.shape[1]
    in_specs = [pl.BlockSpec((rows, q_s.shape[1]), lambda i: (i, 0)),
                pl.BlockSpec((nbb, wb, kw), lambda i: (i, 0, 0)),
                pl.BlockSpec((nbb, wb, kw), lambda i: (i, 0, 0)),
                pl.BlockSpec((rows, kw), lambda i: (rb0 + i, 0)),
                pl.BlockSpec((rows, kw), lambda i: (rb0 + i, 0)),
                _const_spec(sinks.shape)]
    return pl.pallas_call(
        functools.partial(_swa_sample_kernel, dec_seq=dec_seq), grid=(dec_batch // nbb,), in_specs=in_specs,
        out_specs=pl.BlockSpec((rows, q_s.shape[1]), lambda i: (i, 0)),
        out_shape=jax.ShapeDtypeStruct(q_s.shape, F32),
        compiler_params=_params(("arbitrary",)), name="swa_sample")(q_s, wk, wv, k_all, v_all, sinks)


def _out_mlp_kernel(*refs, n_h, n_act, n_p, split_out):
    h_refs = refs[:n_h]
    pos = n_h
    act_refs = []
    for na in n_act:
        act_refs.append(refs[pos:pos + na])
        pos += na
    wo_ref, g_ref, wu_ref, wd_ref = refs[pos:pos + 4]
    pos += 4
    n_o = 2 if split_out else 1
    out_refs = refs[pos:pos + n_o]
    h1_ref, hn_ref, acc_ref = refs[pos + n_o:]
    i = pl.program_id(0)
    j = pl.program_id(1)
    d = h1_ref.shape[1]

    @pl.when(j == 0)
    def _():
        mix = None
        for s, ar in enumerate(act_refs):
            t = _dot(_row_load(ar, i, n_p, BF16), wo_ref[s * d:(s + 1) * d, :])
            mix = t if mix is None else mix + t
        h1 = _row_load(h_refs, i, n_p, F32) + mix
        h1_ref[...] = h1
        hn_ref[...] = (h1 * lax.rsqrt(jnp.mean(h1 * h1, axis=-1, keepdims=True) + EPS) * g_ref[...]).astype(BF16)
        acc_ref[...] = jnp.zeros_like(acc_ref)

    u = jnp.maximum(_dot(hn_ref[...], wu_ref[...]), 0.0)
    acc_ref[...] += _dot((u * u).astype(BF16), wd_ref[...])

    @pl.when(j == pl.num_programs(1) - 1)
    def _():
        res = h1_ref[...] + acc_ref[...]
        if split_out:
            @pl.when(i < n_p)
            def _():
                out_refs[0][...] = res

            @pl.when(i >= n_p)
            def _():
                out_refs[1][...] = res
        else:
            out_refs[0][...] = res


def _out_mlp(h_src, act_srcs, w_out, gain, w_up, w_down, tm, n_p, n_t, split_out):
    d = gain.shape[-1]
    d_ff = w_up.shape[1]
    tf = min(1024, d_ff)
    h_specs, h_args = _row_specs(h_src, tm, n_p)
    in_specs, args, n_act = list(h_specs), list(h_args), []
    for a in act_srcs:
        sp, ar = _row_specs(a, tm, n_p)
        in_specs += sp
        args += ar
        n_act.append(len(ar))
    in_specs += [_const_spec(w_out.shape), _const_spec((1, d)),
                 pl.BlockSpec((d, tf), lambda i, j: (0, j)), pl.BlockSpec((tf, d), lambda i, j: (j, 0))]
    args += [w_out, gain.reshape(1, d), w_up, w_down]
    if split_out:
        n_s = n_t - n_p
        out_shape = [jax.ShapeDtypeStruct((n_p * tm, d), F32), jax.ShapeDtypeStruct((n_s * tm, d), F32)]
        out_specs = [pl.BlockSpec((tm, d), lambda i, j: (jnp.minimum(i, n_p - 1), 0)),
                     pl.BlockSpec((tm, d), lambda i, j: (jnp.maximum(i - n_p, 0), 0))]
    else:
        out_shape = [jax.ShapeDtypeStruct((n_t * tm, d), F32)]
        out_specs = [pl.BlockSpec((tm, d), lambda i, j: (i, 0))]
    kern = functools.partial(_out_mlp_kernel, n_h=len(h_args), n_act=tuple(n_act), n_p=n_p, split_out=split_out)
    return pl.pallas_call(
        kern, grid=(n_t, d_ff // tf), in_specs=in_specs, out_specs=out_specs, out_shape=out_shape,
        scratch_shapes=[pltpu.VMEM((tm, d), F32), pltpu.VMEM((tm, d), BF16), pltpu.VMEM((tm, d), F32)],
        compiler_params=_params(("arbitrary", "arbitrary")), name="out_mlp")(*args)


def kernel(x_prompt, x_sample, cache_k, cache_v, page_table, state_conv, state_ssm, cache_win_k, cache_win_v,
           norm_mix, norm_mlp, w_up, w_down, a_w_in, a_conv_w, a_conv_b, a_dt_bias, a_A_log, a_D, a_ssd_norm,
           a_q_norm, a_k_norm, a_lam_q1, a_lam_k1, a_lam_q2, a_lam_k2, a_subln, a_w_out,
           c_w_in, c_q_norm, c_k_norm, c_sinks, c_w_out):
    batch, seq, d = x_prompt.shape
    dec_batch, dec_seq, _ = x_sample.shape
    page = cache_k.shape[2]
    past_len = page_table.shape[1] * page
    t_p, t_s = batch * seq, dec_batch * dec_seq
    assert a_w_in.shape[0] == 1 and c_w_in.shape[0] == 1 and norm_mix.shape[0] == 2, "kernel is written for depth 2"
    assert d % (2 * LANES) == 0 and seq % SSD_CHUNK == 0 and SSD_CHUNK % dec_seq == 0
    assert dec_batch % (SSD_CHUNK // dec_seq) == 0 and cache_win_k.shape[2] == WINDOW

    tm = next(t for t in (512, 256, 128) if t_p % t == 0 and t_s % t == 0 and seq % t == 0 and t % dec_seq == 0)
    n_p, n_t = t_p // tm, (t_p + t_s) // tm

    xp = x_prompt.reshape(t_p, d)
    xs = x_sample.reshape(t_s, d)
    rope = _rope_tables(seq, past_len, dec_seq, tm)
    ii = jnp.arange(LANES)
    bd = jnp.where((ii[:, None] // HEAD) == (ii[None, :] // HEAD), 1.0 / HEAD, 0.0).astype(BF16)

    xw = d
    cw = xw + 2 * SSD_GROUPS * SSD_STATE
    n_h = xw // SSD_HEAD_DIM
    hpg = n_h // SSD_GROUPS
    qw = d
    kvw = DIFF_KV_HEADS * 2 * DIFF_HEAD_DIM
    offs = [0, xw, xw + cw, xw + cw + n_h, xw + cw + n_h + qw, xw + cw + n_h + qw + kvw]
    w_in = a_w_in[0]
    w_z, w_xbc, w_dt, w_q, w_k, w_v = (w_in[:, offs[0]:offs[1]], w_in[:, offs[1]:offs[2]], w_in[:, offs[2]:offs[3]],
                                       w_in[:, offs[3]:offs[4]], w_in[:, offs[4]:offs[5]], w_in[:, offs[5]:])
    w_dt_g = jnp.pad(w_dt.reshape(d, SSD_GROUPS, hpg), ((0, 0), (0, 0), (0, LANES - hpg))).reshape(d, SSD_GROUPS * LANES)
    bf = lambda w: w.astype(BF16)
    segs0 = [
        dict(w=bf(w_z), kind="plain", width=xw, outs=[dict(dtype=BF16)]),
        dict(w=bf(w_xbc), kind="plain", width=cw, outs=[dict(dtype=F32)]),
        dict(w=bf(w_dt_g), kind="plain", width=SSD_GROUPS * LANES, outs=[dict(dtype=F32)]),
        dict(w=bf(w_q), kind="qk", width=qw, gain=a_q_norm[0], scale=DIFF_HEAD_DIM ** -0.5 * LOG2E, outs=[dict(dtype=BF16)]),
        dict(w=bf(w_k), kind="qk", width=kvw, gain=a_k_norm[0], scale=1.0,
             outs=[dict(dtype=F32, rows=DIFF_KV_HEADS), dict(dtype=BF16)]),
        dict(w=bf(w_v), kind="plain", width=kvw, outs=[dict(dtype=F32, rows=DIFF_KV_HEADS), dict(dtype=BF16)]),
    ]
    z, xbc, dt, q0, k_p, k_s, k0b, v_p, v_s, v0b = _norm_proj((xp, xs), norm_mix[0], segs0, rope, bd, tm, n_p, n_t)

    pad_l = lambda v: jnp.pad(v.astype(F32).reshape(SSD_GROUPS, hpg), ((0, 0), (0, LANES - hpg))).reshape(1, SSD_GROUPS * LANES)
    dtb, alog = pad_l(a_dt_bias[0]), pad_l(a_A_log[0])
    d_x = jnp.repeat(a_D[0].astype(F32), SSD_HEAD_DIM).reshape(1, xw)
    nw = a_ssd_norm[0].astype(F32).reshape(1, xw)
    gw = xw // SSD_GROUPS
    eexp = jnp.where(ii[:, None] == (jnp.arange(gw)[None, :] // SSD_HEAD_DIM), 1.0, 0.0).astype(BF16)
    conv_w, conv_b = a_conv_w[0].astype(F32), a_conv_b[0].astype(F32).reshape(1, cw)

    y_p, p_ssm = _ssd_prompt(xbc, dt, z, conv_w, conv_b, dtb, alog, d_x, nw, eexp, batch, seq)
    prev8 = jnp.pad(state_conv[0].astype(F32), ((0, 0), (8 - (SSD_CONV - 1), 0), (0, 0))).reshape(dec_batch * 8, cw)
    assert dec_seq == 8, "sample conv halo layout assumes 8-token sequences"
    y_s, s_ssm = _ssd_sample(xbc, prev8, dt, z, conv_w, conv_b, dtb, alog, d_x, nw, eexp,
                             state_ssm[0].astype(F32), t_p, dec_batch, dec_seq)

    lam_init0 = 0.8 - 0.6 * math.exp(-0.3 * 0)
    lam_vecs = [v[0].astype(F32).reshape(1, DIFF_HEAD_DIM) for v in (a_lam_q1, a_lam_k1, a_lam_q2, a_lam_k2)]
    subln = a_subln[0].astype(F32).reshape(1, LANES)
    o_p = _diff_prompt(q0, k0b, v0b, lam_vecs, subln, batch, seq, lam_init0)
    n_pool = cache_k.shape[1]
    o_s = _diff_paged(q0[t_p:].astype(F32), k_s, v_s, cache_k[0].reshape(n_pool, page * DIFF_KV_HEADS, LANES),
                      cache_v[0].reshape(n_pool, page * DIFF_KV_HEADS, LANES), page_table.astype(jnp.int32), lam_vecs, subln,
                      dec_batch, dec_seq, lam_init0)

    (h1,) = _out_mlp((xp, xs), [(y_p, y_s), (o_p, o_s)], bf(a_w_out[0]), norm_mlp[0], bf(w_up[0]), bf(w_down[0]),
                     tm, n_p, n_t, split_out=False)

    cqw = d
    ckw = SWA_KV_HEADS * SWA_HEAD_DIM
    wc = c_w_in[0]
    segs1 = [
        dict(w=bf(wc[:, :cqw]), kind="qk", width=cqw, gain=c_q_norm[0], scale=SWA_HEAD_DIM ** -0.5 * LOG2E, outs=[dict(dtype=BF16)]),
        dict(w=bf(wc[:, cqw:cqw + ckw]), kind="qk", width=ckw, gain=c_k_norm[0], scale=1.0,
             outs=[dict(dtype=F32), dict(dtype=BF16, dup=True)]),
        dict(w=bf(wc[:, cqw + ckw:]), kind="dupv", width=ckw, outs=[dict(dtype=F32), dict(dtype=BF16, dup=True)]),
    ]
    q1, k1, k1d, v1, v1d = _norm_proj(h1, norm_mix[1], segs1, rope, bd, tm, n_p, n_t)
    sinks = c_sinks[0].astype(F32).reshape(1, -1)
    o1_p = _swa_prompt(q1, k1d, v1d, sinks, batch, seq)
    o1_s = _swa_sample(q1[t_p:].astype(F32), cache_win_k[0].reshape(dec_batch, WINDOW, ckw),
                       cache_win_v[0].reshape(dec_batch, WINDOW, ckw), k1, v1, sinks, t_p, dec_batch, dec_seq)
    y_prompt, y_sample = _out_mlp(h1, [(o1_p, o1_s)], bf(c_w_out[0]), norm_mlp[1], bf(w_up[1]), bf(w_down[1]),
                                  tm, n_p, n_t, split_out=True)

    n_kv = DIFF_KV_HEADS
    p_k = k_p.reshape(1, batch, seq, n_kv, kvw // n_kv)
    p_v = v_p.reshape(1, batch, seq, n_kv, kvw // n_kv)
    s_k = k_s.reshape(1, dec_batch, dec_seq, n_kv, kvw // n_kv)
    s_v = v_s.reshape(1, dec_batch, dec_seq, n_kv, kvw // n_kv)
    keep = SSD_CONV - 1
    tail = lambda a, n: jnp.stack([a[(b + 1) * seq - n:(b + 1) * seq] for b in range(batch)])
    p_conv = tail(xbc, keep)[None]
    s_conv = jnp.concatenate([state_conv[0].astype(F32), xbc[t_p:].reshape(dec_batch, dec_seq, cw)], axis=1)[:, -keep:][None]
    wk_p = min(WINDOW, seq)
    p_wk = tail(k1, wk_p).reshape(1, batch, wk_p, SWA_KV_HEADS, SWA_HEAD_DIM)
    p_wv = tail(v1, wk_p).reshape(1, batch, wk_p, SWA_KV_HEADS, SWA_HEAD_DIM)
    k1s = k1[t_p:].reshape(dec_batch, dec_seq, SWA_KV_HEADS, SWA_HEAD_DIM)
    v1s = v1[t_p:].reshape(dec_batch, dec_seq, SWA_KV_HEADS, SWA_HEAD_DIM)
    wb = cache_win_k.shape[2]
    s_wk = jnp.concatenate([cache_win_k[0].astype(F32), k1s], axis=1)[:, -wb:][None]
    s_wv = jnp.concatenate([cache_win_v[0].astype(F32), v1s], axis=1)[:, -wb:][None]
    return (y_prompt.reshape(batch, seq, d), y_sample.reshape(dec_batch, dec_seq, d),
            p_k, p_v, s_k, s_v, p_conv, s_conv, p_ssm[None], s_ssm[None], p_wk, p_wv, s_wk, s_wv)
```

```python
import functools
import math

import jax
import jax.numpy as jnp
from jax import lax
from jax.experimental import pallas as pl
from jax.experimental.pallas import tpu as pltpu

F32 = jnp.float32
BF16 = jnp.bfloat16

SSD_HEAD_DIM = 64
SSD_GROUPS = 2
SSD_STATE = 128
SSD_CONV = 4
SSD_CHUNK = 128
DIFF_HEAD_DIM = 64
DIFF_KV_HEADS = 4
SWA_HEAD_DIM = 64
SWA_KV_HEADS = 4
WINDOW = 128
ROPE_THETA = 500000.0
ROPE_ROT = 16
EPS = 1e-6
LOG2E = math.log2(math.e)
MAX_LOG2_JUMP = 64.0

LANES = 128
HEAD = 64
VMEM_LIMIT = 56 * 1024 * 1024


def _dot(a, b):
    return jnp.dot(a, b, preferred_element_type=F32)


def _dot_nt(a, b):
    return lax.dot_general(a, b, (((1,), (1,)), ((), ())), preferred_element_type=F32)


def _split2_dot(v, m):
    hi = v.astype(BF16)
    lo = (v - hi.astype(F32)).astype(BF16)
    return _dot(hi, m) + _dot(lo, m)


def _split3_dot(m, v):
    hi = v.astype(BF16)
    r1 = v - hi.astype(F32)
    mid = r1.astype(BF16)
    lo = (r1 - mid.astype(F32)).astype(BF16)
    return _dot(m, hi) + _dot(m, mid) + _dot(m, lo)


def _silu(x):
    return x * (1.0 / (1.0 + jnp.exp(-x)))


def _softplus(x):
    return jnp.maximum(x, 0.0) + jnp.log(1.0 + jnp.exp(-jnp.abs(x)))


def _lane_half(shape):
    return (lax.broadcasted_iota(jnp.int32, shape, len(shape) - 1) % LANES) // HEAD


def _params(sem, vmem=VMEM_LIMIT):
    return pltpu.CompilerParams(dimension_semantics=sem, vmem_limit_bytes=vmem)


def _const_spec(shape):
    nd = len(shape)
    return pl.BlockSpec(shape, lambda *_: (0,) * nd)


def _row_specs(src, tm, n_p):
    if isinstance(src, tuple):
        p, s = src
        wp = p.shape[1]
        return [pl.BlockSpec((tm, wp), lambda i, *_: (jnp.minimum(i, n_p - 1), 0)),
                pl.BlockSpec((tm, wp), lambda i, *_: (jnp.maximum(i - n_p, 0), 0))], [p, s]
    return [pl.BlockSpec((tm, src.shape[1]), lambda i, *_: (i, 0))], [src]


def _row_load(refs, i, n_p, dtype):
    if len(refs) == 2:
        return jnp.where(i < n_p, refs[0][...].astype(dtype), refs[1][...].astype(dtype))
    return refs[0][...].astype(dtype)


def _rope_tables(seq, past_len, dec_seq, tm):
    half = ROPE_ROT // 2
    inv_freq = ROPE_THETA ** (-jnp.arange(half, dtype=F32) * (2.0 / ROPE_ROT))
    pos = jnp.concatenate([jnp.arange(seq, dtype=jnp.int32),
                           past_len + (jnp.arange(tm, dtype=jnp.int32) % dec_seq)]).astype(F32)
    ang = pos[:, None] * inv_freq[None, :]
    cos, sin = jnp.cos(ang), jnp.sin(ang)
    d = jnp.arange(LANES) % HEAD
    cos_l = jnp.where(d < ROPE_ROT, cos[:, d % half], 1.0)
    sin_a = jnp.where(d < half, -sin[:, d % half], 0.0)
    sin_b = jnp.where((d >= half) & (d < ROPE_ROT), sin[:, d % half], 0.0)
    return cos_l.astype(F32), sin_a.astype(F32), sin_b.astype(F32)


def _head_norm_rope(x, gain, bd, cos_l, sin_a, sin_b, scale):
    ms = _dot((x * x).astype(BF16), bd)
    y = x * lax.rsqrt(ms + EPS) * gain
    half = ROPE_ROT // 2
    y = y * cos_l + pltpu.roll(y, LANES - half, 1) * sin_a + pltpu.roll(y, half, 1) * sin_b
    if scale != 1.0:
        y = y * scale
    return y


def _norm_proj_kernel(*refs, n_src, n_p, segs, n_out):
    src = refs[:n_src]
    g_ref, cos_ref, sa_ref, sb_ref, bd_ref = refs[n_src:n_src + 5]
    pos = n_src + 5
    w_refs = refs[pos:pos + len(segs)]
    pos += len(segs)
    n_gain = sum(1 for s in segs if s["kind"] == "qk")
    gain_refs = refs[pos:pos + n_gain]
    pos += n_gain
    out_refs = refs[pos:pos + n_out]

    i = pl.program_id(0)
    x = _row_load(src, i, n_p, F32)
    xn = (x * lax.rsqrt(jnp.mean(x * x, axis=-1, keepdims=True) + EPS) * g_ref[...]).astype(BF16)

    oi = 0
    gi = 0
    tm = x.shape[0]
    head_rows = []
    for s, w_ref in zip(segs, w_refs):
        width = s["width"]
        n_refs = sum(2 if spec.get("rows") else 1 for spec in s["outs"])
        outs = out_refs[oi:oi + n_refs]
        oi += n_refs
        if s["kind"] == "qk":
            gain = gain_refs[gi][...]
            gi += 1
        step = min(width, 512)
        lane_wise = s["kind"] != "plain" or any(spec.get("rows") or spec.get("dup") for spec in s["outs"])
        for c0 in range(0, width, step):
            acc = _dot(xn, w_ref[:, c0:c0 + step])
            if not lane_wise:
                for o_ref, spec in zip(outs, s["outs"]):
                    o_ref[:, c0:c0 + step] = acc.astype(spec["dtype"])
                continue
            for l0 in range(0, step, LANES):
                y = acc[:, l0:l0 + LANES]
                if s["kind"] == "qk":
                    y = _head_norm_rope(y, gain, bd_ref[...], cos_ref[...], sa_ref[...], sb_ref[...], s["scale"])
                col = c0 + l0
                ri = 0
                for spec in s["outs"]:
                    o_ref = outs[ri]
                    if spec.get("rows"):
                        head_rows.append((outs[ri], outs[ri + 1], pl.ds(col // LANES, tm, stride=spec["rows"]), y))
                        ri += 2
                        continue
                    ri += 1
                    if spec.get("dup"):
                        sw = pltpu.roll(y, HEAD, 1)
                        first = _lane_half(y.shape) == 0
                        o_ref[:, 2 * col:2 * col + LANES] = jnp.where(first, y, sw).astype(spec["dtype"])
                        o_ref[:, 2 * col + LANES:2 * col + 2 * LANES] = jnp.where(first, sw, y).astype(spec["dtype"])
                    else:
                        o_ref[:, col:col + LANES] = y.astype(spec["dtype"])

    if head_rows:
        @pl.when(i < n_p)
        def _():
            for op_ref, _, tgt, y in head_rows:
                op_ref[tgt, :] = y.astype(op_ref.dtype)

        @pl.when(i >= n_p)
        def _():
            for _, os_ref, tgt, y in head_rows:
                os_ref[tgt, :] = y.astype(os_ref.dtype)


def _norm_proj(src, gain, segs, rope, bd, tm, n_p, n_t):
    d_model = gain.shape[-1]
    row_specs, row_args = _row_specs(src, tm, n_p)
    cos_l, sin_a, sin_b = rope
    n_pb = (cos_l.shape[0] - tm) // tm
    rope_spec = pl.BlockSpec((tm, LANES), lambda i: (jnp.where(i < n_p, i % n_pb, n_pb), 0))
    in_specs = row_specs + [_const_spec((1, d_model)), rope_spec, rope_spec, rope_spec, _const_spec((LANES, LANES))]
    args = row_args + [gain.reshape(1, d_model), cos_l, sin_a, sin_b, bd]
    for s in segs:
        in_specs.append(_const_spec(s["w"].shape))
        args.append(s["w"])
    for s in segs:
        if s["kind"] == "qk":
            in_specs.append(_const_spec((1, LANES)))
            args.append(jnp.tile(s["gain"].astype(F32), LANES // HEAD).reshape(1, LANES))
    out_shapes, out_specs = [], []
    for s in segs:
        for spec in s["outs"]:
            if spec.get("rows"):
                nr = spec["rows"]
                assert s["width"] == nr * LANES
                out_shapes += [jax.ShapeDtypeStruct((n_p * tm * nr, LANES), spec["dtype"]),
                               jax.ShapeDtypeStruct(((n_t - n_p) * tm * nr, LANES), spec["dtype"])]
                out_specs += [pl.BlockSpec((tm * nr, LANES), lambda i: (jnp.minimum(i, n_p - 1), 0)),
                              pl.BlockSpec((tm * nr, LANES), lambda i: (jnp.maximum(i - n_p, 0), 0))]
                continue
            w = s["width"] * (2 if spec.get("dup") else 1)
            out_shapes.append(jax.ShapeDtypeStruct((n_t * tm, w), spec["dtype"]))
            out_specs.append(pl.BlockSpec((tm, w), lambda i: (i, 0)))
    kern = functools.partial(
        _norm_proj_kernel, n_src=len(row_args), n_p=n_p,
        segs=[{k: v for k, v in s.items() if k not in ("w", "gain")} for s in segs], n_out=len(out_shapes))
    return pl.pallas_call(
        kern, grid=(n_t,), in_specs=in_specs, out_specs=out_specs, out_shape=out_shapes,
        compiler_params=_params(("arbitrary",)), name="norm_proj")(*args)


def _conv_rolls(u, prevpad, w4, bias, l_sub):
    rows = lax.broadcasted_iota(jnp.int32, u.shape, 0) % l_sub
    n = u.shape[0]
    y = bias + w4[SSD_CONV - 1:SSD_CONV, :] * u
    for k in range(SSD_CONV - 1):
        j = SSD_CONV - 1 - k
        cur = pltpu.roll(u, j, 0)
        prv = pltpu.roll(prevpad, n + j - 8, 0)
        y = y + w4[k:k + 1, :] * jnp.where(rows >= j, cur, prv)
    return y


def _ssd_block(xs_c, bm, cm, dt_raw, z, dtb, a_log, d_x, nw, eexp, hprev_fn, hstore_fn, l_sub):
    n = xs_c.shape[0]
    n_seq = n // l_sub
    hpg = xs_c.shape[1] // SSD_HEAD_DIM
    row = lax.broadcasted_iota(jnp.int32, (n, n), 0)
    col = lax.broadcasted_iota(jnp.int32, (n, n), 1)
    same = (row // l_sub) == (col // l_sub)
    tri = same & (col <= row)

    dt = _softplus(dt_raw + dtb)
    a = dt * (-jnp.exp(a_log))
    a_cum = _split3_dot(jnp.where(tri, 1.0, 0.0).astype(BF16), a)
    a_tot = _split3_dot(jnp.where(same, 1.0, 0.0).astype(BF16), a)
    a_cum_t = a_cum.T
    e_tot_t = jnp.exp(a_tot.T)

    xdt = xs_c * _split2_dot(dt, eexp)
    bm_b = bm.astype(BF16)
    cm_b = cm.astype(BF16)
    g_cb = _dot_nt(cm_b, bm_b)

    half = _lane_half((n, LANES))
    y_parts = []
    for pr in range(hpg // 2):
        ms = []
        for hh in range(2):
            h = 2 * pr + hh
            diff = a_cum[:, h:h + 1] - a_cum_t[h:h + 1, :]
            ms.append((g_cb * jnp.exp(jnp.where(tri, diff, -jnp.inf))).astype(BF16))
        xp = xdt[:, pr * LANES:(pr + 1) * LANES]
        xcat = jnp.concatenate([jnp.where(half == 0, xp, 0.0), jnp.where(half == 1, xp, 0.0)], axis=0).astype(BF16)
        y_parts.append(_dot(jnp.concatenate(ms, axis=1), xcat))
    y = jnp.concatenate(y_parts, axis=1)

    xw_t = (xdt * _split2_dot(jnp.exp(a_tot - a_cum), eexp)).T.astype(BF16)
    seq_row = lax.broadcasted_iota(jnp.int32, (n, 1), 0) // l_sub
    y_off = jnp.zeros_like(y)
    for b in range(n_seq):
        h_b = hprev_fn(b)
        r_b = _dot_nt(cm_b, h_b.reshape(hpg * SSD_HEAD_DIM, SSD_STATE).astype(BF16))
        bm_sel = bm_b if n_seq == 1 else jnp.where(seq_row == b, bm, 0.0).astype(BF16)
        s_b = _dot(xw_t, bm_sel)
        y_off = r_b if n_seq == 1 else jnp.where(seq_row == b, r_b, y_off)
        t0 = b * l_sub
        for h in range(hpg):
            hstore_fn(b, h, e_tot_t[h:h + 1, t0:t0 + 1] * h_b[h] + s_b[h * SSD_HEAD_DIM:(h + 1) * SSD_HEAD_DIM, :])
    y = y + y_off * _split2_dot(jnp.exp(a_cum), eexp) + xs_c * d_x

    gated = y * _silu(z.astype(F32))
    return (gated * lax.rsqrt(jnp.mean(gated * gated, axis=-1, keepdims=True) + EPS) * nw).astype(BF16)


def _ssd_prompt_kernel(xs_ref, b_ref, c_ref, hxs_ref, hb_ref, hc_ref, dt_ref, z_ref,
                       wxs_ref, wb_ref, wc_ref, bxs_ref, bb_ref, bc_ref,
                       dtb_ref, alog_ref, dx_ref, nw_ref, eexp_ref, y_ref, st_ref):
    c = pl.program_id(2)

    @pl.when(c == 0)
    def _():
        st_ref[...] = jnp.zeros_like(st_ref)

    keep = jnp.where(c > 0, 1.0, 0.0)

    def conv(u_ref, h_ref, w_ref, bias_ref):
        u = u_ref[...]
        prevpad = jnp.concatenate([h_ref[...] * keep, jnp.zeros((u.shape[0] - 8, u.shape[1]), F32)], axis=0)
        return _silu(_conv_rolls(u, prevpad, w_ref[...], bias_ref[...], u.shape[0]))

    xs_c = conv(xs_ref, hxs_ref, wxs_ref, bxs_ref)
    bm = conv(b_ref, hb_ref, wb_ref, bb_ref)
    cm = conv(c_ref, hc_ref, wc_ref, bc_ref)

    def hprev(b):
        return st_ref[0]

    def hstore(b, h, val):
        st_ref[0, h] = val

    y_ref[...] = _ssd_block(xs_c, bm, cm, dt_ref[...], z_ref[...], dtb_ref[...], alog_ref[...], dx_ref[...],
                            nw_ref[...], eexp_ref[...], hprev, hstore, xs_c.shape[0])


def _ssd_sample_kernel(xs_ref, b_ref, c_ref, pxs_ref, pb_ref, pc_ref, dt_ref, z_ref,
                       wxs_ref, wb_ref, wc_ref, bxs_ref, bb_ref, bc_ref,
                       dtb_ref, alog_ref, dx_ref, nw_ref, eexp_ref, st_in_ref, y_ref, st_ref, *, l_sub):
    def conv(u_ref, p_ref, w_ref, bias_ref):
        return _silu(_conv_rolls(u_ref[...], p_ref[...], w_ref[...], bias_ref[...], l_sub))

    xs_c = conv(xs_ref, pxs_ref, wxs_ref, bxs_ref)
    bm = conv(b_ref, pb_ref, wb_ref, bb_ref)
    cm = conv(c_ref, pc_ref, wc_ref, bc_ref)

    def hprev(b):
        return st_in_ref[b]

    def hstore(b, h, val):
        st_ref[b, h] = val

    y_ref[...] = _ssd_block(xs_c, bm, cm, dt_ref[...], z_ref[...], dtb_ref[...], alog_ref[...], dx_ref[...],
                            nw_ref[...], eexp_ref[...], hprev, hstore, l_sub)


def _ssd_common_specs(xw, gw, sw, row_of, ng):
    nb = xw // LANES
    def rs(w, colf):
        return pl.BlockSpec((SSD_CHUNK, w), lambda *g: (row_of(*g), colf(g[1])))
    return dict(
        xs=rs(gw, lambda g: g), b=rs(sw, lambda g: nb + g), c=rs(sw, lambda g: nb + ng + g),
        wxs=pl.BlockSpec((SSD_CONV, gw), lambda *g: (0, g[1])),
        wb=pl.BlockSpec((SSD_CONV, sw), lambda *g: (0, nb + g[1])),
        wc=pl.BlockSpec((SSD_CONV, sw), lambda *g: (0, nb + ng + g[1])),
        bxs=pl.BlockSpec((1, gw), lambda *g: (0, g[1])),
        bb=pl.BlockSpec((1, sw), lambda *g: (0, nb + g[1])),
        bc=pl.BlockSpec((1, sw), lambda *g: (0, nb + ng + g[1])),
        lane=pl.BlockSpec((1, LANES), lambda *g: (0, g[1])),
        grp=pl.BlockSpec((1, gw), lambda *g: (0, g[1])),
        eexp=_const_spec((LANES, gw)),
    )


def _ssd_prompt(xbc, dt, z, conv_w, conv_b, dtb, alog, d_x, nw, eexp, batch, seq):
    ng = SSD_GROUPS
    xw = z.shape[1]
    gw, sw = xw // ng, SSD_STATE
    hpg = gw // SSD_HEAD_DIM
    nc = seq // SSD_CHUNK
    row_of = lambda b, g, c: b * nc + c
    sp = _ssd_common_specs(xw, gw, sw, row_of, ng)
    nb = xw // LANES
    sub = SSD_CHUNK // 8

    def halo(w, colf):
        return pl.BlockSpec((8, w), lambda b, g, c: (jnp.maximum((b * nc + c) * sub - 1, 0), colf(g)))

    in_specs = [sp["xs"], sp["b"], sp["c"],
                halo(gw, lambda g: g), halo(sw, lambda g: nb + g), halo(sw, lambda g: nb + ng + g),
                pl.BlockSpec((SSD_CHUNK, LANES), lambda b, g, c: (row_of(b, g, c), g)),
                pl.BlockSpec((SSD_CHUNK, gw), lambda b, g, c: (row_of(b, g, c), g)),
                sp["wxs"], sp["wb"], sp["wc"], sp["bxs"], sp["bb"], sp["bc"],
                sp["lane"], sp["lane"], sp["grp"], sp["grp"], sp["eexp"]]
    out_specs = [pl.BlockSpec((SSD_CHUNK, gw), lambda b, g, c: (row_of(b, g, c), g)),
                 pl.BlockSpec((1, hpg, SSD_HEAD_DIM, SSD_STATE), lambda b, g, c: (b, g, 0, 0))]
    out_shape = [jax.ShapeDtypeStruct((batch * seq, xw), BF16),
                 jax.ShapeDtypeStruct((batch, ng * hpg, SSD_HEAD_DIM, SSD_STATE), F32)]
    return pl.pallas_call(
        _ssd_prompt_kernel, grid=(batch, ng, nc), in_specs=in_specs, out_specs=out_specs, out_shape=out_shape,
        compiler_params=_params(("arbitrary", "arbitrary", "arbitrary")), name="ssd_prompt")(
            xbc, xbc, xbc, xbc, xbc, xbc, dt, z, conv_w, conv_w, conv_w, conv_b, conv_b, conv_b,
            dtb, alog, d_x, nw, eexp)


def _ssd_sample(xbc, prev8, dt, z, conv_w, conv_b, dtb, alog, d_x, nw, eexp, state, t_p, dec_batch, dec_seq):
    ng = SSD_GROUPS
    xw = z.shape[1]
    gw, sw = xw // ng, SSD_STATE
    hpg = gw // SSD_HEAD_DIM
    n_seq = SSD_CHUNK // dec_seq
    nblk = dec_batch // n_seq
    rb0 = t_p // SSD_CHUNK
    row_of = lambda i, g: rb0 + i
    sp = _ssd_common_specs(xw, gw, sw, row_of, ng)
    nb = xw // LANES

    def prev(w, colf):
        return pl.BlockSpec((SSD_CHUNK, w), lambda i, g: (i, colf(g)))

    st_spec = pl.BlockSpec((n_seq, hpg, SSD_HEAD_DIM, SSD_STATE), lambda i, g: (i, g, 0, 0))
    in_specs = [sp["xs"], sp["b"], sp["c"],
                prev(gw, lambda g: g), prev(sw, lambda g: nb + g), prev(sw, lambda g: nb + ng + g),
                pl.BlockSpec((SSD_CHUNK, LANES), lambda i, g: (rb0 + i, g)),
                pl.BlockSpec((SSD_CHUNK, gw), lambda i, g: (rb0 + i, g)),
                sp["wxs"], sp["wb"], sp["wc"], sp["bxs"], sp["bb"], sp["bc"],
                sp["lane"], sp["lane"], sp["grp"], sp["grp"], sp["eexp"], st_spec]
    out_specs = [pl.BlockSpec((SSD_CHUNK, gw), lambda i, g: (i, g)), st_spec]
    out_shape = [jax.ShapeDtypeStruct((dec_batch * dec_seq, xw), BF16),
                 jax.ShapeDtypeStruct(state.shape, F32)]
    return pl.pallas_call(
        functools.partial(_ssd_sample_kernel, l_sub=dec_seq), grid=(nblk, ng), in_specs=in_specs,
        out_specs=out_specs, out_shape=out_shape,
        compiler_params=_params(("arbitrary", "arbitrary")), name="ssd_sample")(
            xbc, xbc, xbc, prev8, prev8, prev8, dt, z, conv_w, conv_w, conv_w, conv_b, conv_b, conv_b,
            dtb, alog, d_x, nw, eexp, state)


def _lambda(lq1, lk1, lq2, lk2, lam_init):
    s1 = jnp.sum(lq1[...] * lk1[...], axis=-1, keepdims=True)
    s2 = jnp.sum(lq2[...] * lk2[...], axis=-1, keepdims=True)
    return jnp.exp(s1) - jnp.exp(s2) + lam_init


def _subln(att, w, lam_init):
    return att * lax.rsqrt(jnp.mean(att * att, axis=-1, keepdims=True) + EPS) * w * (1.0 - lam_init)


def _lane_tile(x, n):
    return x if n == 1 else jnp.concatenate([x] * n, axis=1)


def _diff_prompt_kernel(q_ref, k_ref, v_ref, lq1, lk1, lq2, lk2, sub_ref, o_ref,
                        lhs_ref, vext_ref, s_ref, m_ref, acc_ref, *, tq, tk, lam_init):
    qb = pl.program_id(2)
    n_g = q_ref.shape[1] // LANES
    rows = 2 * n_g * tq
    nl = tk // LANES

    @pl.when(qb == 0)
    def _():
        vext_ref[:, :LANES] = v_ref[...]
        vext_ref[:, LANES:] = jnp.ones((v_ref.shape[0], LANES), vext_ref.dtype)

    half = _lane_half((tq, LANES))
    for j in range(2):
        for g in range(n_g):
            qg = q_ref[:, g * LANES:(g + 1) * LANES]
            r0 = (j * n_g + g) * tq
            lhs_ref[r0:r0 + tq, :] = jnp.where(half == j, qg, jnp.zeros_like(qg))
    m_ref[...] = jnp.full(m_ref.shape, -jnp.inf, F32)
    acc_ref[...] = jnp.zeros(acc_ref.shape, F32)
    n_t = (qb * tq) // tk + 1

    def scores(t, masked):
        k = k_ref[pl.ds(pl.multiple_of(t * tk, tk), tk), :]
        s = _dot_nt(lhs_ref[...], k)
        if masked:
            qpos = qb * tq + lax.broadcasted_iota(jnp.int32, (rows, tk), 0) % tq
            kpos = t * tk + lax.broadcasted_iota(jnp.int32, (rows, tk), 1)
            s = jnp.where(kpos <= qpos, s, -jnp.inf)
        s_ref[t] = s
        mt = s[:, :LANES]
        for c in range(1, nl):
            mt = jnp.maximum(mt, s[:, c * LANES:(c + 1) * LANES])
        m_ref[...] = jnp.maximum(m_ref[...], mt)

    lax.fori_loop(0, n_t - 1, lambda t, c: (scores(t, False), c)[1], 0)
    scores(n_t - 1, True)
    m_ref[...] = jnp.broadcast_to(jnp.max(m_ref[...], axis=-1, keepdims=True), m_ref.shape)

    def attend(t, c):
        p = jnp.exp2(s_ref[t] - _lane_tile(m_ref[...], nl)).astype(BF16)
        acc_ref[...] += _dot(p, vext_ref[pl.ds(pl.multiple_of(t * tk, tk), tk), :])
        return c

    lax.fori_loop(0, n_t, attend, 0)

    lam = _lambda(lq1, lk1, lq2, lk2, lam_init)
    acc = acc_ref[...]
    o = acc[:, :LANES] / acc[:, LANES:]
    att = o[:rows // 2] - lam * o[rows // 2:]
    res = _subln(att, sub_ref[...], lam_init)
    for g in range(n_g):
        o_ref[:, g * LANES:(g + 1) * LANES] = res[g * tq:(g + 1) * tq].astype(o_ref.dtype)


def _diff_prompt(q, kb, vb, lam_vecs, subln, batch, seq, lam_init):
    t_p = batch * seq
    n_kvh = DIFF_KV_HEADS
    qw = q.shape[1] // n_kvh
    kw = kb.shape[1] // n_kvh
    tq = min(512, seq)
    tk = min(512, seq)
    assert seq % tk == 0 and tk % tq == 0 and kw == LANES
    nq = seq // tq
    rows = 2 * (qw // LANES) * tq
    vec = _const_spec((1, DIFF_HEAD_DIM))
    in_specs = [pl.BlockSpec((tq, qw), lambda b, h, i: (b * nq + i, h)),
                pl.BlockSpec((seq, kw), lambda b, h, i: (b, h)),
                pl.BlockSpec((seq, kw), lambda b, h, i: (b, h)),
                vec, vec, vec, vec, _const_spec((1, LANES))]
    return pl.pallas_call(
        functools.partial(_diff_prompt_kernel, tq=tq, tk=tk, lam_init=lam_init),
        grid=(batch, n_kvh, nq), in_specs=in_specs,
        out_specs=pl.BlockSpec((tq, qw), lambda b, h, i: (b * nq + i, h)),
        out_shape=jax.ShapeDtypeStruct((t_p, q.shape[1]), BF16),
        scratch_shapes=[pltpu.VMEM((rows, LANES), BF16), pltpu.VMEM((seq, 2 * LANES), BF16),
                        pltpu.VMEM((seq // tk, rows, tk), F32), pltpu.VMEM((rows, LANES), F32),
                        pltpu.VMEM((rows, 2 * LANES), F32)],
        compiler_params=_params(("arbitrary", "arbitrary", "arbitrary")), name="diff_attn_prompt")(
            q, kb, vb, *lam_vecs, subln)


def _diff_paged_kernel(pt_ref, q_ref, kn_ref, vn_ref, ck_hbm, cv_hbm, lq1, lk1, lq2, lk2, sub_ref,
                       h_ref, a1_ref, a2_ref, wo_ref, g_ref, wu_ref, wd_ref, o_ref, ho_ref,
                       kbuf, vbuf, sem, qst_ref, m_ref, acc_ref, h1_ref, hn_ref, macc_ref,
                       *, n_pg, dec_seq, lam_init, n_j, n_mlp):
    p = pl.program_id(1)
    n_steps = pl.num_programs(1)
    step = pl.program_id(0) * n_steps + p
    last_step = pl.num_programs(0) * n_steps - 1
    slot = step % 2
    n_kvh = DIFF_KV_HEADS
    n_g = q_ref.shape[1] // (n_kvh * LANES)
    rph = n_g * 2 * dec_seq
    rows = n_kvh * rph

    def page_copies(s, sl, lookup):
        bb, pp = s // n_steps, s % n_steps
        out = []
        for i in range(n_pg):
            pg = pt_ref[bb, pp * n_pg + i] if lookup else 0
            out.append(pltpu.make_async_copy(ck_hbm.at[pg], kbuf.at[sl, i], sem.at[sl]))
            out.append(pltpu.make_async_copy(cv_hbm.at[pg], vbuf.at[sl, i], sem.at[sl]))
        return out

    @pl.when(step == 0)
    def _():
        for c in page_copies(step, slot, True):
            c.start()

    @pl.when(step < last_step)
    def _():
        for c in page_copies(step + 1, 1 - slot, True):
            c.start()

    for c in page_copies(step, slot, False):
        c.wait()
    k_pages = [kbuf.at[slot, i] for i in range(n_pg)]
    v_pages = [vbuf.at[slot, i] for i in range(n_pg)]

    own_lane = (lax.broadcasted_iota(jnp.int32, (rows, LANES), 1) % n_kvh
                == lax.broadcasted_iota(jnp.int32, (rows, LANES), 0) // rph)
    lane_bias = jnp.where(own_lane, 0.0, -jnp.inf)

    def scores(qst, k):
        return _dot_nt(qst, k.astype(BF16))

    def block_max(s, mt=None):
        for c in range(s.shape[1] // LANES):
            blk = s[:, c * LANES:(c + 1) * LANES]
            mt = blk if mt is None else jnp.maximum(mt, blk)
        return mt

    def probs_times_v(s, shift, v):
        pr = jnp.exp2(s + _lane_tile(shift, s.shape[1] // LANES)).astype(BF16)
        return _dot(pr, jnp.concatenate([v.astype(BF16), jnp.ones(v.shape, BF16)], axis=1))

    def exact_update(s, bias, v, lane_ok):
        m_old = m_ref[...]
        m_new = jnp.maximum(m_old, jnp.max(jnp.where(lane_ok, block_max(s), -jnp.inf), axis=-1, keepdims=True))
        acc_ref[...] = acc_ref[...] * _lane_tile(jnp.exp2(m_old - m_new), 2) + probs_times_v(s, bias - m_new, v)
        m_ref[...] = m_new

    @pl.when(p == 0)
    def _():
        half = _lane_half((dec_seq, LANES))
        parts = []
        for h in range(n_kvh):
            for g in range(n_g):
                qg = q_ref[:, (h * n_g + g) * LANES:(h * n_g + g + 1) * LANES]
                for j in range(2):
                    parts.append(jnp.where(half == j, qg, 0.0))
        q0 = jnp.concatenate(parts, axis=0).astype(BF16)
        qst_ref[...] = q0
        m0 = jnp.max(jnp.where(own_lane, block_max(scores(q0, k_pages[0][...])), -jnp.inf), axis=-1, keepdims=True)
        m_ref[...] = jnp.broadcast_to(m0, m_ref.shape)
        acc_ref[...] = jnp.zeros(acc_ref.shape, F32)

    mlp_j = step % n_j
    mlp_on = step < n_mlp
    d = h1_ref.shape[1]

    @pl.when(mlp_on & (mlp_j == 0))
    def _():
        h1 = h_ref[...] + _dot(a1_ref[...], wo_ref[0:d, :]) + _dot(a2_ref[...], wo_ref[d:2 * d, :])
        h1_ref[...] = h1
        hn_ref[...] = (h1 * lax.rsqrt(jnp.mean(h1 * h1, axis=-1, keepdims=True) + EPS) * g_ref[...]).astype(BF16)
        macc_ref[...] = jnp.zeros_like(macc_ref)

    def mlp_chunk():
        u = jnp.maximum(_dot(hn_ref[...], wu_ref[...]), 0.0)
        macc_ref[...] += _dot((u * u).astype(BF16), wd_ref[...])

    if n_mlp == pl.num_programs(0) * pl.num_programs(1):
        mlp_chunk()
    else:
        pl.when(mlp_on)(mlp_chunk)

    qst = qst_ref[...]
    m_old = m_ref[...]
    shift = lane_bias - m_old
    mt, pv = None, None
    for i in range(n_pg):
        s = scores(qst, k_pages[i][...])
        mt = block_max(s, mt)
        t = probs_times_v(s, shift, v_pages[i][...])
        pv = t if pv is None else pv + t
    m_step = jnp.max(jnp.where(own_lane, mt, -jnp.inf), axis=-1, keepdims=True)
    in_range = jnp.max(m_step - m_old) <= MAX_LOG2_JUMP

    @pl.when(in_range)
    def _():
        m_new = jnp.maximum(m_old, m_step)
        acc_ref[...] = (acc_ref[...] + pv) * _lane_tile(jnp.exp2(m_old - m_new), 2)
        m_ref[...] = m_new

    @pl.when(jnp.logical_not(in_range))
    def _():
        for i in range(n_pg):
            exact_update(scores(qst, k_pages[i][...]), lane_bias, v_pages[i][...], own_lane)

    @pl.when(mlp_on & (mlp_j == n_j - 1))
    def _():
        ho_ref[...] = h1_ref[...] + macc_ref[...]

    @pl.when(p == n_steps - 1)
    def _():
        lam = _lambda(lq1, lk1, lq2, lk2, lam_init)
        pad = jnp.zeros((LANES - n_kvh * dec_seq, LANES), F32)
        kn = jnp.concatenate([kn_ref[...], pad], axis=0)
        vn = jnp.concatenate([vn_ref[...], pad], axis=0)
        ri = lax.broadcasted_iota(jnp.int32, (rows, LANES), 0)
        ci = lax.broadcasted_iota(jnp.int32, (rows, LANES), 1)
        ok = (ci % n_kvh == ri // rph) & (ci // n_kvh <= ri % dec_seq)
        exact_update(scores(qst, kn), jnp.where(ok, 0.0, -jnp.inf), vn, ok)
        acc = acc_ref[...]
        o = acc[:, :LANES] / acc[:, LANES:]
        for h in range(n_kvh):
            for g in range(n_g):
                r0 = h * rph + g * 2 * dec_seq
                att = o[r0:r0 + dec_seq] - lam * o[r0 + dec_seq:r0 + 2 * dec_seq]
                c0 = (h * n_g + g) * LANES
                o_ref[:, c0:c0 + LANES] = _subln(att, sub_ref[...], lam_init).astype(o_ref.dtype)


def _diff_paged(q_s, k_new, v_new, cache_k, cache_v, page_table, lam_vecs, subln, dec_batch, dec_seq, lam_init,
                h_p, act1_p, act2_p, w_out, gain, w_up, w_down, tm):
    n_pages = page_table.shape[1]
    n_pg = math.gcd(n_pages, 16)
    n_steps = n_pages // n_pg
    n_kvh = DIFF_KV_HEADS
    page_rows = cache_k.shape[1]
    n_g = q_s.shape[1] // (n_kvh * LANES)
    rows = n_kvh * n_g * 2 * dec_seq
    nr = n_kvh * dec_seq
    assert nr <= LANES and k_new.shape == (dec_batch * nr, LANES)
    vec = _const_spec((1, DIFF_HEAD_DIM))

    t_p, d = h_p.shape
    d_ff = w_up.shape[1]
    n_tiles = t_p // tm
    total = dec_batch * n_steps
    n_j = total // n_tiles
    assert n_j >= 1, "fewer page steps than prompt row tiles"
    while d_ff % n_j or (d_ff // n_j) % LANES:
        n_j -= 1
    tf = d_ff // n_j
    n_mlp = n_tiles * n_j

    def tile_of(b, p):
        return jnp.minimum((b * n_steps + p) // n_j, n_tiles - 1)

    def chunk_of(b, p):
        return (b * n_steps + p) % n_j

    row_spec = lambda w: pl.BlockSpec((tm, w), lambda b, p, pt: (tile_of(b, p), 0))
    hbm = pl.BlockSpec(memory_space=pl.ANY)
    in_specs = [pl.BlockSpec((dec_seq, q_s.shape[1]), lambda b, p, pt: (b, 0)),
                pl.BlockSpec((nr, LANES), lambda b, p, pt: (b, 0)),
                pl.BlockSpec((nr, LANES), lambda b, p, pt: (b, 0)),
                hbm, hbm, vec, vec, vec, vec, _const_spec((1, LANES)),
                row_spec(d), row_spec(act1_p.shape[1]), row_spec(act2_p.shape[1]),
                _const_spec(w_out.shape), _const_spec((1, d)),
                pl.BlockSpec((d, tf), lambda b, p, pt: (0, chunk_of(b, p))),
                pl.BlockSpec((tf, d), lambda b, p, pt: (chunk_of(b, p), 0))]
    grid_spec = pltpu.PrefetchScalarGridSpec(
        num_scalar_prefetch=1, grid=(dec_batch, n_steps), in_specs=in_specs,
        out_specs=[pl.BlockSpec((dec_seq, q_s.shape[1]), lambda b, p, pt: (b, 0)), row_spec(d)],
        scratch_shapes=[pltpu.VMEM((2, n_pg, page_rows, LANES), cache_k.dtype),
                        pltpu.VMEM((2, n_pg, page_rows, LANES), cache_v.dtype),
                        pltpu.SemaphoreType.DMA((2,)),
                        pltpu.VMEM((rows, LANES), BF16), pltpu.VMEM((rows, LANES), F32),
                        pltpu.VMEM((rows, 2 * LANES), F32),
                        pltpu.VMEM((tm, d), F32), pltpu.VMEM((tm, d), BF16), pltpu.VMEM((tm, d), F32)])
    return pl.pallas_call(
        functools.partial(_diff_paged_kernel, n_pg=n_pg, dec_seq=dec_seq, lam_init=lam_init, n_j=n_j, n_mlp=n_mlp),
        grid_spec=grid_spec,
        out_shape=[jax.ShapeDtypeStruct(q_s.shape, F32), jax.ShapeDtypeStruct((t_p, d), F32)],
        compiler_params=_params(("arbitrary", "arbitrary")), name="diff_attn_paged")(
            page_table, q_s, k_new, v_new, cache_k, cache_v, *lam_vecs, subln,
            h_p, act1_p, act2_p, w_out, gain.reshape(1, d), w_up, w_down)


def _sink_attend(lhs, blocks, sink):
    ss = [_dot_nt(lhs, k2) + bias for k2, _, bias in blocks]
    mt = None
    for s in ss:
        for c in range(s.shape[1] // LANES):
            blk = s[:, c * LANES:(c + 1) * LANES]
            mt = blk if mt is None else jnp.maximum(mt, blk)
    m = jnp.maximum(jnp.max(mt, axis=-1, keepdims=True), sink)
    acc = None
    for s, (_, v2, _) in zip(ss, blocks):
        p = jnp.exp2(s - _lane_tile(m, s.shape[1] // LANES)).astype(BF16)
        t = _dot(p, jnp.concatenate([v2, jnp.ones(v2.shape, BF16)], axis=1))
        acc = t if acc is None else acc + t
    return acc[:, :LANES] / (acc[:, LANES:] + jnp.exp2(sink - m))


def _swa_prompt_kernel(q_ref, kc_ref, kp_ref, vc_ref, vp_ref, sink_ref, o_ref):
    n = pl.program_id(1)
    w = q_ref.shape[0]
    n_kvh = SWA_KV_HEADS
    n_g = q_ref.shape[1] // (n_kvh * HEAD)
    half = _lane_half((w, LANES))
    qi = lax.broadcasted_iota(jnp.int32, (w, 2 * w), 0)
    ci = lax.broadcasted_iota(jnp.int32, (w, 2 * w), 1)
    ok = (ci > qi) & (ci <= qi + w) & ((ci >= w) | (n > 0))
    bias1 = jnp.where(ok, 0.0, -jnp.inf)
    bias = jnp.concatenate([bias1] * n_g, axis=0)
    sink_all = sink_ref[...] * LOG2E
    for h in range(n_kvh):
        cols = slice(h * LANES, (h + 1) * LANES)
        k2 = jnp.concatenate([kp_ref[:, cols], kc_ref[:, cols]], axis=0)
        v2 = jnp.concatenate([vp_ref[:, cols], vc_ref[:, cols]], axis=0)
        parts, sinks = [], []
        for g in range(n_g):
            hd = h * n_g + g
            qg = q_ref[:, (hd // 2) * LANES:(hd // 2 + 1) * LANES]
            parts.append(jnp.where(half == hd % 2, qg, jnp.zeros_like(qg)))
            sinks.append(jnp.broadcast_to(sink_all[0:1, hd:hd + 1], (w, LANES)))
        o2 = _sink_attend(jnp.concatenate(parts, axis=0), [(k2, v2, bias)], jnp.concatenate(sinks, axis=0))
        for a in range(n_g // 2):
            hd = h * n_g + 2 * a
            pair = jnp.where(half == 0, o2[2 * a * w:(2 * a + 1) * w], o2[(2 * a + 1) * w:(2 * a + 2) * w])
            o_ref[:, (hd // 2) * LANES:(hd // 2 + 1) * LANES] = pair.astype(o_ref.dtype)


def _swa_prompt(q, k2, v2, sinks, batch, seq):
    w = WINDOW
    nb = seq // w
    kw = k2.shape[1]
    cur = pl.BlockSpec((w, kw), lambda b, n: (b * nb + n, 0))
    prv = pl.BlockSpec((w, kw), lambda b, n: (jnp.maximum(b * nb + n - 1, 0), 0))
    in_specs = [pl.BlockSpec((w, q.shape[1]), lambda b, n: (b * nb + n, 0)), cur, prv, cur, prv,
                _const_spec(sinks.shape)]
    return pl.pallas_call(
        _swa_prompt_kernel, grid=(batch, nb), in_specs=in_specs,
        out_specs=pl.BlockSpec((w, q.shape[1]), lambda b, n: (b * nb + n, 0)),
        out_shape=jax.ShapeDtypeStruct((batch * seq, q.shape[1]), BF16),
        compiler_params=_params(("arbitrary", "arbitrary")), name="swa_prompt")(q, k2, k2, v2, v2, sinks)


def _swa_sample_kernel(q_ref, wk_ref, wv_ref, kn_ref, vn_ref, sink_ref, o_ref, *, dec_seq):
    nbb, wb, kw = wk_ref.shape
    n_kvh = SWA_KV_HEADS
    n_h = q_ref.shape[1] // HEAD
    n_g = n_h // n_kvh
    rows = n_h * dec_seq
    nv = kw // LANES
    half8 = _lane_half((dec_seq, LANES))
    qi = lax.broadcasted_iota(jnp.int32, (rows, wb), 0) % dec_seq
    ci = lax.broadcasted_iota(jnp.int32, (rows, wb), 1)
    bias_c = jnp.where(ci > qi, 0.0, -jnp.inf)
    bias_n = jnp.where(ci <= qi, 0.0, -jnp.inf)
    pad = jnp.zeros((wb - dec_seq, kw), F32)
    zero8 = jnp.zeros((dec_seq, LANES), F32)
    sink_all = sink_ref[...] * LOG2E
    sink = jnp.concatenate([jnp.broadcast_to(sink_all[0:1, hd:hd + 1], (dec_seq, LANES)) for hd in range(n_h)], axis=0)

    for bb in range(nbb):
        r0 = bb * dec_seq
        q8 = q_ref[r0:r0 + dec_seq, :]
        pieces = []
        for hd in range(n_h):
            kvh = hd // n_g
            x = jnp.where(half8 == hd % 2, q8[:, (hd // 2) * LANES:(hd // 2 + 1) * LANES], 0.0)
            if hd % 2 != kvh % 2:
                x = pltpu.roll(x, HEAD, 1)
            pieces.append(jnp.concatenate([x if c == kvh // 2 else zero8 for c in range(nv)], axis=1))
        qbd = jnp.concatenate(pieces, axis=0).astype(BF16)
        kn = jnp.concatenate([kn_ref[r0:r0 + dec_seq, :], pad], axis=0).astype(BF16)
        vn = jnp.concatenate([vn_ref[r0:r0 + dec_seq, :], pad], axis=0).astype(BF16)
        s_c = _dot_nt(qbd, wk_ref[bb].astype(BF16)) + bias_c
        s_n = _dot_nt(qbd, kn) + bias_n
        m = jnp.maximum(jnp.max(jnp.maximum(s_c, s_n), axis=-1, keepdims=True), sink)
        p_c, p_n = jnp.exp2(s_c - m), jnp.exp2(s_n - m)
        den = jnp.sum(p_c + p_n, axis=-1, keepdims=True) + jnp.exp2(sink - m)
        o_all = (_dot(p_c.astype(BF16), wv_ref[bb].astype(BF16)) + _dot(p_n.astype(BF16), vn)) / _lane_tile(den, nv)
        for pr in range(n_h // 2):
            blks = []
            for e in range(2):
                hd = 2 * pr + e
                kvh = hd // n_g
                blk = o_all[hd * dec_seq:(hd + 1) * dec_seq, (kvh // 2) * LANES:(kvh // 2 + 1) * LANES]
                blks.append(pltpu.roll(blk, HEAD, 1) if kvh % 2 != e else blk)
            o_ref[r0:r0 + dec_seq, pr * LANES:(pr + 1) * LANES] = jnp.where(half8 == 0, blks[0], blks[1])


def _swa_sample(q_s, wk, wv, k_all, v_all, sinks, t_p, dec_batch, dec_seq):
    nbb = math.gcd(dec_batch, 8)
    rows = nbb * dec_seq
    rb0 = t_p // rows
    kw = k_all.shape[1]
    wb = wk.shape[1]
    in_specs = [pl.BlockSpec((rows, q_s.shape[1]), lambda i: (i, 0)),
                pl.BlockSpec((nbb, wb, kw), lambda i: (i, 0, 0)),
                pl.BlockSpec((nbb, wb, kw), lambda i: (i, 0, 0)),
                pl.BlockSpec((rows, kw), lambda i: (rb0 + i, 0)),
                pl.BlockSpec((rows, kw), lambda i: (rb0 + i, 0)),
                _const_spec(sinks.shape)]
    return pl.pallas_call(
        functools.partial(_swa_sample_kernel, dec_seq=dec_seq), grid=(dec_batch // nbb,), in_specs=in_specs,
        out_specs=pl.BlockSpec((rows, q_s.shape[1]), lambda i: (i, 0)),
        out_shape=jax.ShapeDtypeStruct(q_s.shape, F32),
        compiler_params=_params(("arbitrary",)), name="swa_sample")(q_s, wk, wv, k_all, v_all, sinks)


def _out_mlp_kernel(*refs, n_h, n_act, n_p, split_out):
    h_refs = refs[:n_h]
    pos = n_h
    act_refs = []
    for na in n_act:
        act_refs.append(refs[pos:pos + na])
        pos += na
    wo_ref, g_ref, wu_ref, wd_ref = refs[pos:pos + 4]
    pos += 4
    n_o = 2 if split_out else 1
    out_refs = refs[pos:pos + n_o]
    h1_ref, hn_ref, acc_ref = refs[pos + n_o:]
    i = pl.program_id(0)
    j = pl.program_id(1)
    d = h1_ref.shape[1]

    @pl.when(j == 0)
    def _():
        mix = None
        for s, ar in enumerate(act_refs):
            t = _dot(_row_load(ar, i, n_p, BF16), wo_ref[s * d:(s + 1) * d, :])
            mix = t if mix is None else mix + t
        h1 = _row_load(h_refs, i, n_p, F32) + mix
        h1_ref[...] = h1
        hn_ref[...] = (h1 * lax.rsqrt(jnp.mean(h1 * h1, axis=-1, keepdims=True) + EPS) * g_ref[...]).astype(BF16)
        acc_ref[...] = jnp.zeros_like(acc_ref)

    u = jnp.maximum(_dot(hn_ref[...], wu_ref[...]), 0.0)
    acc_ref[...] += _dot((u * u).astype(BF16), wd_ref[...])

    @pl.when(j == pl.num_programs(1) - 1)
    def _():
        res = h1_ref[...] + acc_ref[...]
        if split_out:
            @pl.when(i < n_p)
            def _():
                out_refs[0][...] = res

            @pl.when(i >= n_p)
            def _():
                out_refs[1][...] = res
        else:
            out_refs[0][...] = res


def _out_mlp(h_src, act_srcs, w_out, gain, w_up, w_down, tm, n_p, n_t, split_out):
    d = gain.shape[-1]
    d_ff = w_up.shape[1]
    tf = min(1024, d_ff)
    h_specs, h_args = _row_specs(h_src, tm, n_p)
    in_specs, args, n_act = list(h_specs), list(h_args), []
    for a in act_srcs:
        sp, ar = _row_specs(a, tm, n_p)
        in_specs += sp
        args += ar
        n_act.append(len(ar))
    in_specs += [_const_spec(w_out.shape), _const_spec((1, d)),
                 pl.BlockSpec((d, tf), lambda i, j: (0, j)), pl.BlockSpec((tf, d), lambda i, j: (j, 0))]
    args += [w_out, gain.reshape(1, d), w_up, w_down]
    if split_out:
        n_s = n_t - n_p
        out_shape = [jax.ShapeDtypeStruct((n_p * tm, d), F32), jax.ShapeDtypeStruct((n_s * tm, d), F32)]
        out_specs = [pl.BlockSpec((tm, d), lambda i, j: (jnp.minimum(i, n_p - 1), 0)),
                     pl.BlockSpec((tm, d), lambda i, j: (jnp.maximum(i - n_p, 0), 0))]
    else:
        out_shape = [jax.ShapeDtypeStruct((n_t * tm, d), F32)]
        out_specs = [pl.BlockSpec((tm, d), lambda i, j: (i, 0))]
    kern = functools.partial(_out_mlp_kernel, n_h=len(h_args), n_act=tuple(n_act), n_p=n_p, split_out=split_out)
    return pl.pallas_call(
        kern, grid=(n_t, d_ff // tf), in_specs=in_specs, out_specs=out_specs, out_shape=out_shape,
        scratch_shapes=[pltpu.VMEM((tm, d), F32), pltpu.VMEM((tm, d), BF16), pltpu.VMEM((tm, d), F32)],
        compiler_params=_params(("arbitrary", "arbitrary")), name="out_mlp")(*args)


def kernel(x_prompt, x_sample, cache_k, cache_v, page_table, state_conv, state_ssm, cache_win_k, cache_win_v,
           norm_mix, norm_mlp, w_up, w_down, a_w_in, a_conv_w, a_conv_b, a_dt_bias, a_A_log, a_D, a_ssd_norm,
           a_q_norm, a_k_norm, a_lam_q1, a_lam_k1, a_lam_q2, a_lam_k2, a_subln, a_w_out,
           c_w_in, c_q_norm, c_k_norm, c_sinks, c_w_out):
    batch, seq, d = x_prompt.shape
    dec_batch, dec_seq, _ = x_sample.shape
    page = cache_k.shape[2]
    past_len = page_table.shape[1] * page
    t_p, t_s = batch * seq, dec_batch * dec_seq
    assert a_w_in.shape[0] == 1 and c_w_in.shape[0] == 1 and norm_mix.shape[0] == 2, "kernel is written for depth 2"
    assert d % (2 * LANES) == 0 and seq % SSD_CHUNK == 0 and SSD_CHUNK % dec_seq == 0
    assert dec_batch % (SSD_CHUNK // dec_seq) == 0 and cache_win_k.shape[2] == WINDOW

    tm = next(t for t in (512, 256, 128) if t_p % t == 0 and t_s % t == 0 and seq % t == 0 and t % dec_seq == 0)
    n_p, n_t = t_p // tm, (t_p + t_s) // tm

    xp = x_prompt.reshape(t_p, d)
    xs = x_sample.reshape(t_s, d)
    rope = _rope_tables(seq, past_len, dec_seq, tm)
    ii = jnp.arange(LANES)
    bd = jnp.where((ii[:, None] // HEAD) == (ii[None, :] // HEAD), 1.0 / HEAD, 0.0).astype(BF16)

    xw = d
    cw = xw + 2 * SSD_GROUPS * SSD_STATE
    n_h = xw // SSD_HEAD_DIM
    hpg = n_h // SSD_GROUPS
    qw = d
    kvw = DIFF_KV_HEADS * 2 * DIFF_HEAD_DIM
    offs = [0, xw, xw + cw, xw + cw + n_h, xw + cw + n_h + qw, xw + cw + n_h + qw + kvw]
    w_in = a_w_in[0]
    w_z, w_xbc, w_dt, w_q, w_k, w_v = (w_in[:, offs[0]:offs[1]], w_in[:, offs[1]:offs[2]], w_in[:, offs[2]:offs[3]],
                                       w_in[:, offs[3]:offs[4]], w_in[:, offs[4]:offs[5]], w_in[:, offs[5]:])
    w_dt_g = jnp.pad(w_dt.reshape(d, SSD_GROUPS, hpg), ((0, 0), (0, 0), (0, LANES - hpg))).reshape(d, SSD_GROUPS * LANES)
    bf = lambda w: w.astype(BF16)
    segs0 = [
        dict(w=bf(w_z), kind="plain", width=xw, outs=[dict(dtype=BF16)]),
        dict(w=bf(w_xbc), kind="plain", width=cw, outs=[dict(dtype=F32)]),
        dict(w=bf(w_dt_g), kind="plain", width=SSD_GROUPS * LANES, outs=[dict(dtype=F32)]),
        dict(w=bf(w_q), kind="qk", width=qw, gain=a_q_norm[0], scale=DIFF_HEAD_DIM ** -0.5 * LOG2E, outs=[dict(dtype=BF16)]),
        dict(w=bf(w_k), kind="qk", width=kvw, gain=a_k_norm[0], scale=1.0,
             outs=[dict(dtype=F32, rows=DIFF_KV_HEADS), dict(dtype=BF16)]),
        dict(w=bf(w_v), kind="plain", width=kvw, outs=[dict(dtype=F32, rows=DIFF_KV_HEADS), dict(dtype=BF16)]),
    ]
    z, xbc, dt, q0, k_p, k_s, k0b, v_p, v_s, v0b = _norm_proj((xp, xs), norm_mix[0], segs0, rope, bd, tm, n_p, n_t)

    pad_l = lambda v: jnp.pad(v.astype(F32).reshape(SSD_GROUPS, hpg), ((0, 0), (0, LANES - hpg))).reshape(1, SSD_GROUPS * LANES)
    dtb, alog = pad_l(a_dt_bias[0]), pad_l(a_A_log[0])
    d_x = jnp.repeat(a_D[0].astype(F32), SSD_HEAD_DIM).reshape(1, xw)
    nw = a_ssd_norm[0].astype(F32).reshape(1, xw)
    gw = xw // SSD_GROUPS
    eexp = jnp.where(ii[:, None] == (jnp.arange(gw)[None, :] // SSD_HEAD_DIM), 1.0, 0.0).astype(BF16)
    conv_w, conv_b = a_conv_w[0].astype(F32), a_conv_b[0].astype(F32).reshape(1, cw)

    y_p, p_ssm = _ssd_prompt(xbc, dt, z, conv_w, conv_b, dtb, alog, d_x, nw, eexp, batch, seq)
    prev8 = jnp.pad(state_conv[0].astype(F32), ((0, 0), (8 - (SSD_CONV - 1), 0), (0, 0))).reshape(dec_batch * 8, cw)
    assert dec_seq == 8, "sample conv halo layout assumes 8-token sequences"
    y_s, s_ssm = _ssd_sample(xbc, prev8, dt, z, conv_w, conv_b, dtb, alog, d_x, nw, eexp,
                             state_ssm[0].astype(F32), t_p, dec_batch, dec_seq)

    lam_init0 = 0.8 - 0.6 * math.exp(-0.3 * 0)
    lam_vecs = [v[0].astype(F32).reshape(1, DIFF_HEAD_DIM) for v in (a_lam_q1, a_lam_k1, a_lam_q2, a_lam_k2)]
    subln = a_subln[0].astype(F32).reshape(1, LANES)
    o_p = _diff_prompt(q0, k0b, v0b, lam_vecs, subln, batch, seq, lam_init0)
    n_pool = cache_k.shape[1]
    w_out0, w_up0, w_down0 = bf(a_w_out[0]), bf(w_up[0]), bf(w_down[0])
    o_s, h1_p = _diff_paged(q0[t_p:].astype(F32), k_s, v_s, cache_k[0].reshape(n_pool, page * DIFF_KV_HEADS, LANES),
                            cache_v[0].reshape(n_pool, page * DIFF_KV_HEADS, LANES), page_table.astype(jnp.int32),
                            lam_vecs, subln, dec_batch, dec_seq, lam_init0,
                            xp, y_p, o_p, w_out0, norm_mlp[0], w_up0, w_down0, tm)
    (h1_s,) = _out_mlp(xs, [y_s, o_s], w_out0, norm_mlp[0], w_up0, w_down0, tm, n_t - n_p, n_t - n_p, split_out=False)
    h1 = (h1_p, h1_s)

    cqw = d
    ckw = SWA_KV_HEADS * SWA_HEAD_DIM
    wc = c_w_in[0]
    segs1 = [
        dict(w=bf(wc[:, :cqw]), kind="qk", width=cqw, gain=c_q_norm[0], scale=SWA_HEAD_DIM ** -0.5 * LOG2E, outs=[dict(dtype=BF16)]),
        dict(w=bf(wc[:, cqw:cqw + ckw]), kind="qk", width=ckw, gain=c_k_norm[0], scale=1.0,
             outs=[dict(dtype=F32), dict(dtype=BF16, dup=True)]),
        dict(w=bf(wc[:, cqw + ckw:]), kind="dupv", width=ckw, outs=[dict(dtype=F32), dict(dtype=BF16, dup=True)]),
    ]
    q1, k1, k1d, v1, v1d = _norm_proj(h1, norm_mix[1], segs1, rope, bd, tm, n_p, n_t)
    sinks = c_sinks[0].astype(F32).reshape(1, -1)
    o1_p = _swa_prompt(q1, k1d, v1d, sinks, batch, seq)
    o1_s = _swa_sample(q1[t_p:].astype(F32), cache_win_k[0].reshape(dec_batch, WINDOW, ckw),
                       cache_win_v[0].reshape(dec_batch, WINDOW, ckw), k1, v1, sinks, t_p, dec_batch, dec_seq)
    y_prompt, y_sample = _out_mlp(h1, [(o1_p, o1_s)], bf(c_w_out[0]), norm_mlp[1], bf(w_up[1]), bf(w_down[1]),
                                  tm, n_p, n_t, split_out=True)

    n_kv = DIFF_KV_HEADS
    p_k = k_p.reshape(1, batch, seq, n_kv, kvw // n_kv)
    p_v = v_p.reshape(1, batch, seq, n_kv, kvw // n_kv)
    s_k = k_s.reshape(1, dec_batch, dec_seq, n_kv, kvw // n_kv)
    s_v = v_s.reshape(1, dec_batch, dec_seq, n_kv, kvw // n_kv)
    keep = SSD_CONV - 1
    tail = lambda a, n: jnp.stack([a[(b + 1) * seq - n:(b + 1) * seq] for b in range(batch)])
    p_conv = tail(xbc, keep)[None]
    s_conv = jnp.concatenate([state_conv[0].astype(F32), xbc[t_p:].reshape(dec_batch, dec_seq, cw)], axis=1)[:, -keep:][None]
    wk_p = min(WINDOW, seq)
    p_wk = tail(k1, wk_p).reshape(1, batch, wk_p, SWA_KV_HEADS, SWA_HEAD_DIM)
    p_wv = tail(v1, wk_p).reshape(1, batch, wk_p, SWA_KV_HEADS, SWA_HEAD_DIM)
    k1s = k1[t_p:].reshape(dec_batch, dec_seq, SWA_KV_HEADS, SWA_HEAD_DIM)
    v1s = v1[t_p:].reshape(dec_batch, dec_seq, SWA_KV_HEADS, SWA_HEAD_DIM)
    wb = cache_win_k.shape[2]
    s_wk = jnp.concatenate([cache_win_k[0].astype(F32), k1s], axis=1)[:, -wb:][None]
    s_wv = jnp.concatenate([cache_win_v[0].astype(F32), v1s], axis=1)[:, -wb:][None]
    return (y_prompt.reshape(batch, seq, d), y_sample.reshape(dec_batch, dec_seq, d),
            p_k, p_v, s_k, s_v, p_conv, s_conv, p_ssm[None], s_ssm[None], p_wk, p_wv, s_wk, s_wv)
```

```python
import functools
import math

import jax
import jax.numpy as jnp
from jax import lax
from jax.experimental import pallas as pl
from jax.experimental.pallas import tpu as pltpu

F32 = jnp.float32
BF16 = jnp.bfloat16

SSD_HEAD_DIM = 64
SSD_GROUPS = 2
SSD_STATE = 128
SSD_CONV = 4
SSD_CHUNK = 128
DIFF_HEAD_DIM = 64
DIFF_KV_HEADS = 4
SWA_HEAD_DIM = 64
SWA_KV_HEADS = 4
WINDOW = 128
ROPE_THETA = 500000.0
ROPE_ROT = 16
EPS = 1e-6
LOG2E = math.log2(math.e)
MAX_LOG2_JUMP = 64.0

LANES = 128
HEAD = 64
VMEM_LIMIT = 56 * 1024 * 1024


def _dot(a, b):
    return jnp.dot(a, b, preferred_element_type=F32)


def _dot_nt(a, b):
    return lax.dot_general(a, b, (((1,), (1,)), ((), ())), preferred_element_type=F32)


def _split2_dot(v, m):
    hi = v.astype(BF16)
    lo = (v - hi.astype(F32)).astype(BF16)
    return _dot(hi, m) + _dot(lo, m)


def _split3_dot(m, v):
    hi = v.astype(BF16)
    r1 = v - hi.astype(F32)
    mid = r1.astype(BF16)
    lo = (r1 - mid.astype(F32)).astype(BF16)
    return _dot(m, hi) + _dot(m, mid) + _dot(m, lo)


def _silu(x):
    return x * (1.0 / (1.0 + jnp.exp(-x)))


def _softplus(x):
    return jnp.maximum(x, 0.0) + jnp.log(1.0 + jnp.exp(-jnp.abs(x)))


def _lane_half(shape):
    return (lax.broadcasted_iota(jnp.int32, shape, len(shape) - 1) % LANES) // HEAD


def _params(sem, vmem=VMEM_LIMIT):
    return pltpu.CompilerParams(dimension_semantics=sem, vmem_limit_bytes=vmem)


def _const_spec(shape):
    nd = len(shape)
    return pl.BlockSpec(shape, lambda *_: (0,) * nd)


def _row_specs(src, tm, n_p):
    if isinstance(src, tuple):
        p, s = src
        wp = p.shape[1]
        return [pl.BlockSpec((tm, wp), lambda i, *_: (jnp.minimum(i, n_p - 1), 0)),
                pl.BlockSpec((tm, wp), lambda i, *_: (jnp.maximum(i - n_p, 0), 0))], [p, s]
    return [pl.BlockSpec((tm, src.shape[1]), lambda i, *_: (i, 0))], [src]


def _row_load(refs, i, n_p, dtype):
    if len(refs) == 2:
        return jnp.where(i < n_p, refs[0][...].astype(dtype), refs[1][...].astype(dtype))
    return refs[0][...].astype(dtype)


def _rope_tables(seq, past_len, dec_seq, tm):
    half = ROPE_ROT // 2
    inv_freq = ROPE_THETA ** (-jnp.arange(half, dtype=F32) * (2.0 / ROPE_ROT))
    pos = jnp.concatenate([jnp.arange(seq, dtype=jnp.int32),
                           past_len + (jnp.arange(tm, dtype=jnp.int32) % dec_seq)]).astype(F32)
    ang = pos[:, None] * inv_freq[None, :]
    cos, sin = jnp.cos(ang), jnp.sin(ang)
    d = jnp.arange(LANES) % HEAD
    cos_l = jnp.where(d < ROPE_ROT, cos[:, d % half], 1.0)
    sin_a = jnp.where(d < half, -sin[:, d % half], 0.0)
    sin_b = jnp.where((d >= half) & (d < ROPE_ROT), sin[:, d % half], 0.0)
    return cos_l.astype(F32), sin_a.astype(F32), sin_b.astype(F32)


def _head_norm_rope(x, gain, bd, cos_l, sin_a, sin_b, scale):
    ms = _dot((x * x).astype(BF16), bd)
    y = x * lax.rsqrt(ms + EPS) * gain
    half = ROPE_ROT // 2
    y = y * cos_l + pltpu.roll(y, LANES - half, 1) * sin_a + pltpu.roll(y, half, 1) * sin_b
    if scale != 1.0:
        y = y * scale
    return y


def _norm_proj_kernel(*refs, n_src, n_p, segs, n_out):
    src = refs[:n_src]
    g_ref, cos_ref, sa_ref, sb_ref, bd_ref = refs[n_src:n_src + 5]
    pos = n_src + 5
    w_refs = refs[pos:pos + len(segs)]
    pos += len(segs)
    n_gain = sum(1 for s in segs if s["kind"] == "qk")
    gain_refs = refs[pos:pos + n_gain]
    pos += n_gain
    out_refs = refs[pos:pos + n_out]

    i = pl.program_id(0)
    x = _row_load(src, i, n_p, F32)
    xn = (x * lax.rsqrt(jnp.mean(x * x, axis=-1, keepdims=True) + EPS) * g_ref[...]).astype(BF16)

    oi = 0
    gi = 0
    tm = x.shape[0]
    head_rows = []
    for s, w_ref in zip(segs, w_refs):
        width = s["width"]
        n_refs = sum(2 if spec.get("rows") else 1 for spec in s["outs"])
        outs = out_refs[oi:oi + n_refs]
        oi += n_refs
        if s["kind"] == "qk":
            gain = gain_refs[gi][...]
            gi += 1
        step = min(width, 512)
        lane_wise = s["kind"] != "plain" or any(spec.get("rows") or spec.get("dup") for spec in s["outs"])
        for c0 in range(0, width, step):
            acc = _dot(xn, w_ref[:, c0:c0 + step])
            if not lane_wise:
                for o_ref, spec in zip(outs, s["outs"]):
                    o_ref[:, c0:c0 + step] = acc.astype(spec["dtype"])
                continue
            for l0 in range(0, step, LANES):
                y = acc[:, l0:l0 + LANES]
                if s["kind"] == "qk":
                    y = _head_norm_rope(y, gain, bd_ref[...], cos_ref[...], sa_ref[...], sb_ref[...], s["scale"])
                col = c0 + l0
                ri = 0
                for spec in s["outs"]:
                    o_ref = outs[ri]
                    if spec.get("rows"):
                        head_rows.append((outs[ri], outs[ri + 1], pl.ds(col // LANES, tm, stride=spec["rows"]), y))
                        ri += 2
                        continue
                    ri += 1
                    if spec.get("dup"):
                        sw = pltpu.roll(y, HEAD, 1)
                        first = _lane_half(y.shape) == 0
                        o_ref[:, 2 * col:2 * col + LANES] = jnp.where(first, y, sw).astype(spec["dtype"])
                        o_ref[:, 2 * col + LANES:2 * col + 2 * LANES] = jnp.where(first, sw, y).astype(spec["dtype"])
                    else:
                        o_ref[:, col:col + LANES] = y.astype(spec["dtype"])

    if head_rows:
        @pl.when(i < n_p)
        def _():
            for op_ref, _, tgt, y in head_rows:
                op_ref[tgt, :] = y.astype(op_ref.dtype)

        @pl.when(i >= n_p)
        def _():
            for _, os_ref, tgt, y in head_rows:
                os_ref[tgt, :] = y.astype(os_ref.dtype)


def _norm_proj(src, gain, segs, rope, bd, tm, n_p, n_t):
    d_model = gain.shape[-1]
    row_specs, row_args = _row_specs(src, tm, n_p)
    cos_l, sin_a, sin_b = rope
    n_pb = (cos_l.shape[0] - tm) // tm
    rope_spec = pl.BlockSpec((tm, LANES), lambda i: (jnp.where(i < n_p, i % n_pb, n_pb), 0))
    in_specs = row_specs + [_const_spec((1, d_model)), rope_spec, rope_spec, rope_spec, _const_spec((LANES, LANES))]
    args = row_args + [gain.reshape(1, d_model), cos_l, sin_a, sin_b, bd]
    for s in segs:
        in_specs.append(_const_spec(s["w"].shape))
        args.append(s["w"])
    for s in segs:
        if s["kind"] == "qk":
            in_specs.append(_const_spec((1, LANES)))
            args.append(jnp.tile(s["gain"].astype(F32), LANES // HEAD).reshape(1, LANES))
    out_shapes, out_specs = [], []
    for s in segs:
        for spec in s["outs"]:
            if spec.get("rows"):
                nr = spec["rows"]
                assert s["width"] == nr * LANES
                out_shapes += [jax.ShapeDtypeStruct((n_p * tm * nr, LANES), spec["dtype"]),
                               jax.ShapeDtypeStruct(((n_t - n_p) * tm * nr, LANES), spec["dtype"])]
                out_specs += [pl.BlockSpec((tm * nr, LANES), lambda i: (jnp.minimum(i, n_p - 1), 0)),
                              pl.BlockSpec((tm * nr, LANES), lambda i: (jnp.maximum(i - n_p, 0), 0))]
                continue
            w = s["width"] * (2 if spec.get("dup") else 1)
            out_shapes.append(jax.ShapeDtypeStruct((n_t * tm, w), spec["dtype"]))
            out_specs.append(pl.BlockSpec((tm, w), lambda i: (i, 0)))
    kern = functools.partial(
        _norm_proj_kernel, n_src=len(row_args), n_p=n_p,
        segs=[{k: v for k, v in s.items() if k not in ("w", "gain")} for s in segs], n_out=len(out_shapes))
    return pl.pallas_call(
        kern, grid=(n_t,), in_specs=in_specs, out_specs=out_specs, out_shape=out_shapes,
        compiler_params=_params(("arbitrary",)), name="norm_proj")(*args)


def _conv_rolls(u, prevpad, w4, bias, l_sub):
    rows = lax.broadcasted_iota(jnp.int32, u.shape, 0) % l_sub
    n = u.shape[0]
    y = bias + w4[SSD_CONV - 1:SSD_CONV, :] * u
    for k in range(SSD_CONV - 1):
        j = SSD_CONV - 1 - k
        cur = pltpu.roll(u, j, 0)
        prv = pltpu.roll(prevpad, n + j - 8, 0)
        y = y + w4[k:k + 1, :] * jnp.where(rows >= j, cur, prv)
    return y


def _ssd_block(xs_c, bm, cm, dt_raw, z, dtb, a_log, d_x, nw, eexp, hprev_fn, hstore_fn, l_sub):
    n = xs_c.shape[0]
    n_seq = n // l_sub
    hpg = xs_c.shape[1] // SSD_HEAD_DIM
    row = lax.broadcasted_iota(jnp.int32, (n, n), 0)
    col = lax.broadcasted_iota(jnp.int32, (n, n), 1)
    same = (row // l_sub) == (col // l_sub)
    tri = same & (col <= row)

    dt = _softplus(dt_raw + dtb)
    a = dt * (-jnp.exp(a_log))
    a_cum = _split3_dot(jnp.where(tri, 1.0, 0.0).astype(BF16), a)
    a_tot = _split3_dot(jnp.where(same, 1.0, 0.0).astype(BF16), a)
    a_cum_t = a_cum.T
    e_tot_t = jnp.exp(a_tot.T)

    xdt = xs_c * _split2_dot(dt, eexp)
    bm_b = bm.astype(BF16)
    cm_b = cm.astype(BF16)
    g_cb = _dot_nt(cm_b, bm_b)

    half = _lane_half((n, LANES))
    y_parts = []
    for pr in range(hpg // 2):
        ms = []
        for hh in range(2):
            h = 2 * pr + hh
            diff = a_cum[:, h:h + 1] - a_cum_t[h:h + 1, :]
            ms.append((g_cb * jnp.exp(jnp.where(tri, diff, -jnp.inf))).astype(BF16))
        xp = xdt[:, pr * LANES:(pr + 1) * LANES]
        xcat = jnp.concatenate([jnp.where(half == 0, xp, 0.0), jnp.where(half == 1, xp, 0.0)], axis=0).astype(BF16)
        y_parts.append(_dot(jnp.concatenate(ms, axis=1), xcat))
    y = jnp.concatenate(y_parts, axis=1)

    xw_t = (xdt * _split2_dot(jnp.exp(a_tot - a_cum), eexp)).T.astype(BF16)
    seq_row = lax.broadcasted_iota(jnp.int32, (n, 1), 0) // l_sub
    y_off = jnp.zeros_like(y)
    for b in range(n_seq):
        h_b = hprev_fn(b)
        r_b = _dot_nt(cm_b, h_b.reshape(hpg * SSD_HEAD_DIM, SSD_STATE).astype(BF16))
        bm_sel = bm_b if n_seq == 1 else jnp.where(seq_row == b, bm, 0.0).astype(BF16)
        s_b = _dot(xw_t, bm_sel)
        y_off = r_b if n_seq == 1 else jnp.where(seq_row == b, r_b, y_off)
        t0 = b * l_sub
        for h in range(hpg):
            hstore_fn(b, h, e_tot_t[h:h + 1, t0:t0 + 1] * h_b[h] + s_b[h * SSD_HEAD_DIM:(h + 1) * SSD_HEAD_DIM, :])
    y = y + y_off * _split2_dot(jnp.exp(a_cum), eexp) + xs_c * d_x

    gated = y * _silu(z.astype(F32))
    return (gated * lax.rsqrt(jnp.mean(gated * gated, axis=-1, keepdims=True) + EPS) * nw).astype(BF16)


def _ssd_prompt_kernel(xs_ref, b_ref, c_ref, hxs_ref, hb_ref, hc_ref, dt_ref, z_ref,
                       wxs_ref, wb_ref, wc_ref, bxs_ref, bb_ref, bc_ref,
                       dtb_ref, alog_ref, dx_ref, nw_ref, eexp_ref, y_ref, st_ref):
    c = pl.program_id(2)

    @pl.when(c == 0)
    def _():
        st_ref[...] = jnp.zeros_like(st_ref)

    keep = jnp.where(c > 0, 1.0, 0.0)

    def conv(u_ref, h_ref, w_ref, bias_ref):
        u = u_ref[...]
        prevpad = jnp.concatenate([h_ref[...] * keep, jnp.zeros((u.shape[0] - 8, u.shape[1]), F32)], axis=0)
        return _silu(_conv_rolls(u, prevpad, w_ref[...], bias_ref[...], u.shape[0]))

    xs_c = conv(xs_ref, hxs_ref, wxs_ref, bxs_ref)
    bm = conv(b_ref, hb_ref, wb_ref, bb_ref)
    cm = conv(c_ref, hc_ref, wc_ref, bc_ref)

    def hprev(b):
        return st_ref[0]

    def hstore(b, h, val):
        st_ref[0, h] = val

    y_ref[...] = _ssd_block(xs_c, bm, cm, dt_ref[...], z_ref[...], dtb_ref[...], alog_ref[...], dx_ref[...],
                            nw_ref[...], eexp_ref[...], hprev, hstore, xs_c.shape[0])


def _ssd_sample_kernel(xs_ref, b_ref, c_ref, pxs_ref, pb_ref, pc_ref, dt_ref, z_ref,
                       wxs_ref, wb_ref, wc_ref, bxs_ref, bb_ref, bc_ref,
                       dtb_ref, alog_ref, dx_ref, nw_ref, eexp_ref, st_in_ref, y_ref, st_ref, *, l_sub):
    def conv(u_ref, p_ref, w_ref, bias_ref):
        return _silu(_conv_rolls(u_ref[...], p_ref[...], w_ref[...], bias_ref[...], l_sub))

    xs_c = conv(xs_ref, pxs_ref, wxs_ref, bxs_ref)
    bm = conv(b_ref, pb_ref, wb_ref, bb_ref)
    cm = conv(c_ref, pc_ref, wc_ref, bc_ref)

    def hprev(b):
        return st_in_ref[b]

    def hstore(b, h, val):
        st_ref[b, h] = val

    y_ref[...] = _ssd_block(xs_c, bm, cm, dt_ref[...], z_ref[...], dtb_ref[...], alog_ref[...], dx_ref[...],
                            nw_ref[...], eexp_ref[...], hprev, hstore, l_sub)


def _ssd_common_specs(xw, gw, sw, row_of, ng):
    nb = xw // LANES
    def rs(w, colf):
        return pl.BlockSpec((SSD_CHUNK, w), lambda *g: (row_of(*g), colf(g[1])))
    return dict(
        xs=rs(gw, lambda g: g), b=rs(sw, lambda g: nb + g), c=rs(sw, lambda g: nb + ng + g),
        wxs=pl.BlockSpec((SSD_CONV, gw), lambda *g: (0, g[1])),
        wb=pl.BlockSpec((SSD_CONV, sw), lambda *g: (0, nb + g[1])),
        wc=pl.BlockSpec((SSD_CONV, sw), lambda *g: (0, nb + ng + g[1])),
        bxs=pl.BlockSpec((1, gw), lambda *g: (0, g[1])),
        bb=pl.BlockSpec((1, sw), lambda *g: (0, nb + g[1])),
        bc=pl.BlockSpec((1, sw), lambda *g: (0, nb + ng + g[1])),
        lane=pl.BlockSpec((1, LANES), lambda *g: (0, g[1])),
        grp=pl.BlockSpec((1, gw), lambda *g: (0, g[1])),
        eexp=_const_spec((LANES, gw)),
    )


def _ssd_prompt(xbc, dt, z, conv_w, conv_b, dtb, alog, d_x, nw, eexp, batch, seq):
    ng = SSD_GROUPS
    xw = z.shape[1]
    gw, sw = xw // ng, SSD_STATE
    hpg = gw // SSD_HEAD_DIM
    nc = seq // SSD_CHUNK
    row_of = lambda b, g, c: b * nc + c
    sp = _ssd_common_specs(xw, gw, sw, row_of, ng)
    nb = xw // LANES
    sub = SSD_CHUNK // 8

    def halo(w, colf):
        return pl.BlockSpec((8, w), lambda b, g, c: (jnp.maximum((b * nc + c) * sub - 1, 0), colf(g)))

    in_specs = [sp["xs"], sp["b"], sp["c"],
                halo(gw, lambda g: g), halo(sw, lambda g: nb + g), halo(sw, lambda g: nb + ng + g),
                pl.BlockSpec((SSD_CHUNK, LANES), lambda b, g, c: (row_of(b, g, c), g)),
                pl.BlockSpec((SSD_CHUNK, gw), lambda b, g, c: (row_of(b, g, c), g)),
                sp["wxs"], sp["wb"], sp["wc"], sp["bxs"], sp["bb"], sp["bc"],
                sp["lane"], sp["lane"], sp["grp"], sp["grp"], sp["eexp"]]
    out_specs = [pl.BlockSpec((SSD_CHUNK, gw), lambda b, g, c: (row_of(b, g, c), g)),
                 pl.BlockSpec((1, hpg, SSD_HEAD_DIM, SSD_STATE), lambda b, g, c: (b, g, 0, 0))]
    out_shape = [jax.ShapeDtypeStruct((batch * seq, xw), BF16),
                 jax.ShapeDtypeStruct((batch, ng * hpg, SSD_HEAD_DIM, SSD_STATE), F32)]
    return pl.pallas_call(
        _ssd_prompt_kernel, grid=(batch, ng, nc), in_specs=in_specs, out_specs=out_specs, out_shape=out_shape,
        compiler_params=_params(("arbitrary", "arbitrary", "arbitrary")), name="ssd_prompt")(
            xbc, xbc, xbc, xbc, xbc, xbc, dt, z, conv_w, conv_w, conv_w, conv_b, conv_b, conv_b,
            dtb, alog, d_x, nw, eexp)


def _ssd_sample(xbc, prev8, dt, z, conv_w, conv_b, dtb, alog, d_x, nw, eexp, state, t_p, dec_batch, dec_seq):
    ng = SSD_GROUPS
    xw = z.shape[1]
    gw, sw = xw // ng, SSD_STATE
    hpg = gw // SSD_HEAD_DIM
    n_seq = SSD_CHUNK // dec_seq
    nblk = dec_batch // n_seq
    rb0 = t_p // SSD_CHUNK
    row_of = lambda i, g: rb0 + i
    sp = _ssd_common_specs(xw, gw, sw, row_of, ng)
    nb = xw // LANES

    def prev(w, colf):
        return pl.BlockSpec((SSD_CHUNK, w), lambda i, g: (i, colf(g)))

    st_spec = pl.BlockSpec((n_seq, hpg, SSD_HEAD_DIM, SSD_STATE), lambda i, g: (i, g, 0, 0))
    in_specs = [sp["xs"], sp["b"], sp["c"],
                prev(gw, lambda g: g), prev(sw, lambda g: nb + g), prev(sw, lambda g: nb + ng + g),
                pl.BlockSpec((SSD_CHUNK, LANES), lambda i, g: (rb0 + i, g)),
                pl.BlockSpec((SSD_CHUNK, gw), lambda i, g: (rb0 + i, g)),
                sp["wxs"], sp["wb"], sp["wc"], sp["bxs"], sp["bb"], sp["bc"],
                sp["lane"], sp["lane"], sp["grp"], sp["grp"], sp["eexp"], st_spec]
    out_specs = [pl.BlockSpec((SSD_CHUNK, gw), lambda i, g: (i, g)), st_spec]
    out_shape = [jax.ShapeDtypeStruct((dec_batch * dec_seq, xw), BF16),
                 jax.ShapeDtypeStruct(state.shape, F32)]
    return pl.pallas_call(
        functools.partial(_ssd_sample_kernel, l_sub=dec_seq), grid=(nblk, ng), in_specs=in_specs,
        out_specs=out_specs, out_shape=out_shape,
        compiler_params=_params(("arbitrary", "arbitrary")), name="ssd_sample")(
            xbc, xbc, xbc, prev8, prev8, prev8, dt, z, conv_w, conv_w, conv_w, conv_b, conv_b, conv_b,
            dtb, alog, d_x, nw, eexp, state)


def _lambda(lq1, lk1, lq2, lk2, lam_init):
    s1 = jnp.sum(lq1[...] * lk1[...], axis=-1, keepdims=True)
    s2 = jnp.sum(lq2[...] * lk2[...], axis=-1, keepdims=True)
    return jnp.exp(s1) - jnp.exp(s2) + lam_init


def _subln(att, w, lam_init):
    return att * lax.rsqrt(jnp.mean(att * att, axis=-1, keepdims=True) + EPS) * w * (1.0 - lam_init)


def _lane_tile(x, n):
    return x if n == 1 else jnp.concatenate([x] * n, axis=1)


def _diff_prompt_kernel(q_ref, k_ref, v_ref, lq1, lk1, lq2, lk2, sub_ref, o_ref,
                        lhs_ref, vext_ref, s_ref, m_ref, acc_ref, *, tq, tk, lam_init):
    qb = pl.program_id(2)
    n_g = q_ref.shape[1] // LANES
    rows = 2 * n_g * tq
    nl = tk // LANES

    @pl.when(qb == 0)
    def _():
        vext_ref[:, :LANES] = v_ref[...]
        vext_ref[:, LANES:] = jnp.ones((v_ref.shape[0], LANES), vext_ref.dtype)

    half = _lane_half((tq, LANES))
    for j in range(2):
        for g in range(n_g):
            qg = q_ref[:, g * LANES:(g + 1) * LANES]
            r0 = (j * n_g + g) * tq
            lhs_ref[r0:r0 + tq, :] = jnp.where(half == j, qg, jnp.zeros_like(qg))
    m_ref[...] = jnp.full(m_ref.shape, -jnp.inf, F32)
    acc_ref[...] = jnp.zeros(acc_ref.shape, F32)
    n_t = (qb * tq) // tk + 1

    def scores(t, masked):
        k = k_ref[pl.ds(pl.multiple_of(t * tk, tk), tk), :]
        s = _dot_nt(lhs_ref[...], k)
        if masked:
            qpos = qb * tq + lax.broadcasted_iota(jnp.int32, (rows, tk), 0) % tq
            kpos = t * tk + lax.broadcasted_iota(jnp.int32, (rows, tk), 1)
            s = jnp.where(kpos <= qpos, s, -jnp.inf)
        s_ref[t] = s
        mt = s[:, :LANES]
        for c in range(1, nl):
            mt = jnp.maximum(mt, s[:, c * LANES:(c + 1) * LANES])
        m_ref[...] = jnp.maximum(m_ref[...], mt)

    lax.fori_loop(0, n_t - 1, lambda t, c: (scores(t, False), c)[1], 0)
    scores(n_t - 1, True)
    m_ref[...] = jnp.broadcast_to(jnp.max(m_ref[...], axis=-1, keepdims=True), m_ref.shape)

    def attend(t, c):
        p = jnp.exp2(s_ref[t] - _lane_tile(m_ref[...], nl)).astype(BF16)
        acc_ref[...] += _dot(p, vext_ref[pl.ds(pl.multiple_of(t * tk, tk), tk), :])
        return c

    lax.fori_loop(0, n_t, attend, 0)

    lam = _lambda(lq1, lk1, lq2, lk2, lam_init)
    acc = acc_ref[...]
    o = acc[:, :LANES] / acc[:, LANES:]
    att = o[:rows // 2] - lam * o[rows // 2:]
    res = _subln(att, sub_ref[...], lam_init)
    for g in range(n_g):
        o_ref[:, g * LANES:(g + 1) * LANES] = res[g * tq:(g + 1) * tq].astype(o_ref.dtype)


def _diff_prompt(q, kb, vb, lam_vecs, subln, batch, seq, lam_init):
    t_p = batch * seq
    n_kvh = DIFF_KV_HEADS
    qw = q.shape[1] // n_kvh
    kw = kb.shape[1] // n_kvh
    tq = min(512, seq)
    tk = min(512, seq)
    assert seq % tk == 0 and tk % tq == 0 and kw == LANES
    nq = seq // tq
    rows = 2 * (qw // LANES) * tq
    vec = _const_spec((1, DIFF_HEAD_DIM))
    in_specs = [pl.BlockSpec((tq, qw), lambda b, h, i: (b * nq + i, h)),
                pl.BlockSpec((seq, kw), lambda b, h, i: (b, h)),
                pl.BlockSpec((seq, kw), lambda b, h, i: (b, h)),
                vec, vec, vec, vec, _const_spec((1, LANES))]
    return pl.pallas_call(
        functools.partial(_diff_prompt_kernel, tq=tq, tk=tk, lam_init=lam_init),
        grid=(batch, n_kvh, nq), in_specs=in_specs,
        out_specs=pl.BlockSpec((tq, qw), lambda b, h, i: (b * nq + i, h)),
        out_shape=jax.ShapeDtypeStruct((t_p, q.shape[1]), BF16),
        scratch_shapes=[pltpu.VMEM((rows, LANES), BF16), pltpu.VMEM((seq, 2 * LANES), BF16),
                        pltpu.VMEM((seq // tk, rows, tk), F32), pltpu.VMEM((rows, LANES), F32),
                        pltpu.VMEM((rows, 2 * LANES), F32)],
        compiler_params=_params(("arbitrary", "arbitrary", "arbitrary")), name="diff_attn_prompt")(
            q, kb, vb, *lam_vecs, subln)


def _diff_paged_kernel(pt_ref, q_ref, kn_ref, vn_ref, ck_hbm, cv_hbm, lq1, lk1, lq2, lk2, sub_ref, min_ref, accin_ref,
                       h_ref, *rest, n_pg, dec_seq, lam_init, n_j, n_mlp, n_act, first, last, step_off):
    act_refs = rest[:n_act]
    (wo_ref, g_ref, wu_ref, wd_ref, o_ref, mout_ref, accout_ref, ho_ref,
     kbuf, vbuf, sem, qst_ref, m_ref, acc_ref, h1_ref, hn_ref, macc_ref) = rest[n_act:]
    p = pl.program_id(1)
    n_steps = pl.num_programs(1)
    step = pl.program_id(0) * n_steps + p
    last_step = pl.num_programs(0) * n_steps - 1
    slot = step % 2
    n_kvh = DIFF_KV_HEADS
    n_g = q_ref.shape[1] // (n_kvh * LANES)
    rph = n_g * 2 * dec_seq
    rows = n_kvh * rph

    def page_copies(s, sl, lookup):
        bb, pp = s // n_steps, s % n_steps
        out = []
        for i in range(n_pg):
            pg = pt_ref[bb, (step_off + pp) * n_pg + i] if lookup else 0
            out.append(pltpu.make_async_copy(ck_hbm.at[pg], kbuf.at[sl, i], sem.at[sl]))
            out.append(pltpu.make_async_copy(cv_hbm.at[pg], vbuf.at[sl, i], sem.at[sl]))
        return out

    @pl.when(step == 0)
    def _():
        for c in page_copies(step, slot, True):
            c.start()

    @pl.when(step < last_step)
    def _():
        for c in page_copies(step + 1, 1 - slot, True):
            c.start()

    for c in page_copies(step, slot, False):
        c.wait()
    k_pages = [kbuf.at[slot, i] for i in range(n_pg)]
    v_pages = [vbuf.at[slot, i] for i in range(n_pg)]

    own_lane = (lax.broadcasted_iota(jnp.int32, (rows, LANES), 1) % n_kvh
                == lax.broadcasted_iota(jnp.int32, (rows, LANES), 0) // rph)
    lane_bias = jnp.where(own_lane, 0.0, -jnp.inf)

    def scores(qst, k):
        return _dot_nt(qst, k.astype(BF16))

    def block_max(s, mt=None):
        for c in range(s.shape[1] // LANES):
            blk = s[:, c * LANES:(c + 1) * LANES]
            mt = blk if mt is None else jnp.maximum(mt, blk)
        return mt

    def probs_times_v(s, shift, v):
        pr = jnp.exp2(s + _lane_tile(shift, s.shape[1] // LANES)).astype(BF16)
        return _dot(pr, jnp.concatenate([v.astype(BF16), jnp.ones(v.shape, BF16)], axis=1))

    def exact_update(s, bias, v, lane_ok):
        m_old = m_ref[...]
        m_new = jnp.maximum(m_old, jnp.max(jnp.where(lane_ok, block_max(s), -jnp.inf), axis=-1, keepdims=True))
        acc_ref[...] = acc_ref[...] * _lane_tile(jnp.exp2(m_old - m_new), 2) + probs_times_v(s, bias - m_new, v)
        m_ref[...] = m_new

    @pl.when(p == 0)
    def _():
        half = _lane_half((dec_seq, LANES))
        parts = []
        for h in range(n_kvh):
            for g in range(n_g):
                qg = q_ref[:, (h * n_g + g) * LANES:(h * n_g + g + 1) * LANES]
                for j in range(2):
                    parts.append(jnp.where(half == j, qg, 0.0))
        q0 = jnp.concatenate(parts, axis=0).astype(BF16)
        qst_ref[...] = q0
        if first:
            m0 = jnp.max(jnp.where(own_lane, block_max(scores(q0, k_pages[0][...])), -jnp.inf), axis=-1, keepdims=True)
            m_ref[...] = jnp.broadcast_to(m0, m_ref.shape)
            acc_ref[...] = jnp.zeros(acc_ref.shape, F32)
        else:
            m_ref[...] = min_ref[...]
            acc_ref[...] = accin_ref[...]

    mlp_j = step % n_j
    mlp_on = step < n_mlp
    d = h1_ref.shape[1]

    @pl.when(mlp_on & (mlp_j == 0))
    def _():
        h1 = h_ref[...]
        for n, a_ref in enumerate(act_refs):
            h1 = h1 + _dot(a_ref[...], wo_ref[n * d:(n + 1) * d, :])
        h1_ref[...] = h1
        hn_ref[...] = (h1 * lax.rsqrt(jnp.mean(h1 * h1, axis=-1, keepdims=True) + EPS) * g_ref[...]).astype(BF16)
        macc_ref[...] = jnp.zeros_like(macc_ref)

    def mlp_chunk():
        u = jnp.maximum(_dot(hn_ref[...], wu_ref[...]), 0.0)
        macc_ref[...] += _dot((u * u).astype(BF16), wd_ref[...])

    if n_mlp == pl.num_programs(0) * pl.num_programs(1):
        mlp_chunk()
    else:
        pl.when(mlp_on)(mlp_chunk)

    qst = qst_ref[...]
    m_old = m_ref[...]
    shift = lane_bias - m_old
    mt, pv = None, None
    for i in range(n_pg):
        s = scores(qst, k_pages[i][...])
        mt = block_max(s, mt)
        t = probs_times_v(s, shift, v_pages[i][...])
        pv = t if pv is None else pv + t
    m_step = jnp.max(jnp.where(own_lane, mt, -jnp.inf), axis=-1, keepdims=True)
    in_range = jnp.max(m_step - m_old) <= MAX_LOG2_JUMP

    @pl.when(in_range)
    def _():
        m_new = jnp.maximum(m_old, m_step)
        acc_ref[...] = (acc_ref[...] + pv) * _lane_tile(jnp.exp2(m_old - m_new), 2)
        m_ref[...] = m_new

    @pl.when(jnp.logical_not(in_range))
    def _():
        for i in range(n_pg):
            exact_update(scores(qst, k_pages[i][...]), lane_bias, v_pages[i][...], own_lane)

    @pl.when(mlp_on & (mlp_j == n_j - 1))
    def _():
        ho_ref[...] = h1_ref[...] + macc_ref[...]

    @pl.when(p == n_steps - 1)
    def _():
        if not last:
            mout_ref[...] = m_ref[...]
            accout_ref[...] = acc_ref[...]
            o_ref[...] = jnp.zeros(o_ref.shape, o_ref.dtype)
            return
        lam = _lambda(lq1, lk1, lq2, lk2, lam_init)
        pad = jnp.zeros((LANES - n_kvh * dec_seq, LANES), F32)
        kn = jnp.concatenate([kn_ref[...], pad], axis=0)
        vn = jnp.concatenate([vn_ref[...], pad], axis=0)
        ri = lax.broadcasted_iota(jnp.int32, (rows, LANES), 0)
        ci = lax.broadcasted_iota(jnp.int32, (rows, LANES), 1)
        ok = (ci % n_kvh == ri // rph) & (ci // n_kvh <= ri % dec_seq)
        exact_update(scores(qst, kn), jnp.where(ok, 0.0, -jnp.inf), vn, ok)
        mout_ref[...] = m_ref[...]
        accout_ref[...] = acc_ref[...]
        acc = acc_ref[...]
        o = acc[:, :LANES] / acc[:, LANES:]
        for h in range(n_kvh):
            for g in range(n_g):
                r0 = h * rph + g * 2 * dec_seq
                att = o[r0:r0 + dec_seq] - lam * o[r0 + dec_seq:r0 + 2 * dec_seq]
                c0 = (h * n_g + g) * LANES
                o_ref[:, c0:c0 + LANES] = _subln(att, sub_ref[...], lam_init).astype(o_ref.dtype)


def _diff_paged(q_s, k_new, v_new, cache_k, cache_v, page_table, lam_vecs, subln, dec_batch, dec_seq, lam_init,
                state, part, n_parts, h_p, acts_p, w_out, gain, w_up, w_down, tm):
    n_pages = page_table.shape[1]
    n_pg = math.gcd(n_pages // n_parts, 16)
    n_steps = n_pages // n_parts // n_pg
    assert n_steps * n_pg * n_parts == n_pages
    n_kvh = DIFF_KV_HEADS
    page_rows = cache_k.shape[1]
    n_g = q_s.shape[1] // (n_kvh * LANES)
    rows = n_kvh * n_g * 2 * dec_seq
    nr = n_kvh * dec_seq
    assert nr <= LANES and k_new.shape == (dec_batch * nr, LANES)
    vec = _const_spec((1, DIFF_HEAD_DIM))

    t_p, d = h_p.shape
    d_ff = w_up.shape[1]
    n_tiles = t_p // tm
    total = dec_batch * n_steps
    n_j = total // n_tiles
    assert n_j >= 1, "fewer page steps than prompt row tiles"
    while d_ff % n_j or (d_ff // n_j) % LANES:
        n_j -= 1
    tf = d_ff // n_j
    n_mlp = n_tiles * n_j

    def tile_of(b, p):
        return jnp.minimum((b * n_steps + p) // n_j, n_tiles - 1)

    def chunk_of(b, p):
        return (b * n_steps + p) % n_j

    row_spec = lambda w: pl.BlockSpec((tm, w), lambda b, p, pt: (tile_of(b, p), 0))
    hbm = pl.BlockSpec(memory_space=pl.ANY)
    in_specs = [pl.BlockSpec((dec_seq, q_s.shape[1]), lambda b, p, pt: (b, 0)),
                pl.BlockSpec((nr, LANES), lambda b, p, pt: (b, 0)),
                pl.BlockSpec((nr, LANES), lambda b, p, pt: (b, 0)),
                hbm, hbm, vec, vec, vec, vec, _const_spec((1, LANES)),
                pl.BlockSpec((rows, LANES), lambda b, p, pt: (b, 0)),
                pl.BlockSpec((rows, 2 * LANES), lambda b, p, pt: (b, 0)),
                row_spec(d)] + [row_spec(a.shape[1]) for a in acts_p] + [
                _const_spec(w_out.shape), _const_spec((1, d)),
                pl.BlockSpec((d, tf), lambda b, p, pt: (0, chunk_of(b, p))),
                pl.BlockSpec((tf, d), lambda b, p, pt: (chunk_of(b, p), 0))]
    grid_spec = pltpu.PrefetchScalarGridSpec(
        num_scalar_prefetch=1, grid=(dec_batch, n_steps), in_specs=in_specs,
        out_specs=[pl.BlockSpec((dec_seq, q_s.shape[1]), lambda b, p, pt: (b, 0)),
                   pl.BlockSpec((rows, LANES), lambda b, p, pt: (b, 0)),
                   pl.BlockSpec((rows, 2 * LANES), lambda b, p, pt: (b, 0)), row_spec(d)],
        scratch_shapes=[pltpu.VMEM((2, n_pg, page_rows, LANES), cache_k.dtype),
                        pltpu.VMEM((2, n_pg, page_rows, LANES), cache_v.dtype),
                        pltpu.SemaphoreType.DMA((2,)),
                        pltpu.VMEM((rows, LANES), BF16), pltpu.VMEM((rows, LANES), F32),
                        pltpu.VMEM((rows, 2 * LANES), F32),
                        pltpu.VMEM((tm, d), F32), pltpu.VMEM((tm, d), BF16), pltpu.VMEM((tm, d), F32)])
    if state is None:
        state = (jnp.zeros((dec_batch * rows, LANES), F32), jnp.zeros((dec_batch * rows, 2 * LANES), F32))
    o, m_out, acc_out, h_out = pl.pallas_call(
        functools.partial(_diff_paged_kernel, n_pg=n_pg, dec_seq=dec_seq, lam_init=lam_init, n_j=n_j, n_mlp=n_mlp,
                          n_act=len(acts_p), first=part == 0, last=part == n_parts - 1, step_off=part * n_steps),
        grid_spec=grid_spec,
        out_shape=[jax.ShapeDtypeStruct(q_s.shape, F32), jax.ShapeDtypeStruct(state[0].shape, F32),
                   jax.ShapeDtypeStruct(state[1].shape, F32), jax.ShapeDtypeStruct((t_p, d), F32)],
        compiler_params=_params(("arbitrary", "arbitrary")), name="diff_attn_paged")(
            page_table, q_s, k_new, v_new, cache_k, cache_v, *lam_vecs, subln, *state,
            h_p, *acts_p, w_out, gain.reshape(1, d), w_up, w_down)
    return o, (m_out, acc_out), h_out


def _sink_attend(lhs, blocks, sink):
    ss = [_dot_nt(lhs, k2) + bias for k2, _, bias in blocks]
    mt = None
    for s in ss:
        for c in range(s.shape[1] // LANES):
            blk = s[:, c * LANES:(c + 1) * LANES]
            mt = blk if mt is None else jnp.maximum(mt, blk)
    m = jnp.maximum(jnp.max(mt, axis=-1, keepdims=True), sink)
    acc = None
    for s, (_, v2, _) in zip(ss, blocks):
        p = jnp.exp2(s - _lane_tile(m, s.shape[1] // LANES)).astype(BF16)
        t = _dot(p, jnp.concatenate([v2, jnp.ones(v2.shape, BF16)], axis=1))
        acc = t if acc is None else acc + t
    return acc[:, :LANES] / (acc[:, LANES:] + jnp.exp2(sink - m))


def _swa_prompt_kernel(q_ref, kc_ref, kp_ref, vc_ref, vp_ref, sink_ref, o_ref):
    n = pl.program_id(1)
    w = q_ref.shape[0]
    n_kvh = SWA_KV_HEADS
    n_g = q_ref.shape[1] // (n_kvh * HEAD)
    half = _lane_half((w, LANES))
    qi = lax.broadcasted_iota(jnp.int32, (w, 2 * w), 0)
    ci = lax.broadcasted_iota(jnp.int32, (w, 2 * w), 1)
    ok = (ci > qi) & (ci <= qi + w) & ((ci >= w) | (n > 0))
    bias1 = jnp.where(ok, 0.0, -jnp.inf)
    bias = jnp.concatenate([bias1] * n_g, axis=0)
    sink_all = sink_ref[...] * LOG2E
    for h in range(n_kvh):
        cols = slice(h * LANES, (h + 1) * LANES)
        k2 = jnp.concatenate([kp_ref[:, cols], kc_ref[:, cols]], axis=0)
        v2 = jnp.concatenate([vp_ref[:, cols], vc_ref[:, cols]], axis=0)
        parts, sinks = [], []
        for g in range(n_g):
            hd = h * n_g + g
            qg = q_ref[:, (hd // 2) * LANES:(hd // 2 + 1) * LANES]
            parts.append(jnp.where(half == hd % 2, qg, jnp.zeros_like(qg)))
            sinks.append(jnp.broadcast_to(sink_all[0:1, hd:hd + 1], (w, LANES)))
        o2 = _sink_attend(jnp.concatenate(parts, axis=0), [(k2, v2, bias)], jnp.concatenate(sinks, axis=0))
        for a in range(n_g // 2):
            hd = h * n_g + 2 * a
            pair = jnp.where(half == 0, o2[2 * a * w:(2 * a + 1) * w], o2[(2 * a + 1) * w:(2 * a + 2) * w])
            o_ref[:, (hd // 2) * LANES:(hd // 2 + 1) * LANES] = pair.astype(o_ref.dtype)


def _swa_prompt(q, k2, v2, sinks, batch, seq):
    w = WINDOW
    nb = seq // w
    kw = k2.shape[1]
    cur = pl.BlockSpec((w, kw), lambda b, n: (b * nb + n, 0))
    prv = pl.BlockSpec((w, kw), lambda b, n: (jnp.maximum(b * nb + n - 1, 0), 0))
    in_specs = [pl.BlockSpec((w, q.shape[1]), lambda b, n: (b * nb + n, 0)), cur, prv, cur, prv,
                _const_spec(sinks.shape)]
    return pl.pallas_call(
        _swa_prompt_kernel, grid=(batch, nb), in_specs=in_specs,
        out_specs=pl.BlockSpec((w, q.shape[1]), lambda b, n: (b * nb + n, 0)),
        out_shape=jax.ShapeDtypeStruct((batch * seq, q.shape[1]), BF16),
        compiler_params=_params(("arbitrary", "arbitrary")), name="swa_prompt")(q, k2, k2, v2, v2, sinks)


def _swa_sample_kernel(q_ref, wk_ref, wv_ref, kn_ref, vn_ref, sink_ref, o_ref, *, dec_seq):
    nbb, wb, kw = wk_ref.shape
    n_kvh = SWA_KV_HEADS
    n_h = q_ref.shape[1] // HEAD
    n_g = n_h // n_kvh
    rows = n_h * dec_seq
    nv = kw // LANES
    half8 = _lane_half((dec_seq, LANES))
    qi = lax.broadcasted_iota(jnp.int32, (rows, wb), 0) % dec_seq
    ci = lax.broadcasted_iota(jnp.int32, (rows, wb), 1)
    bias_c = jnp.where(ci > qi, 0.0, -jnp.inf)
    bias_n = jnp.where(ci <= qi, 0.0, -jnp.inf)
    pad = jnp.zeros((wb - dec_seq, kw), F32)
    zero8 = jnp.zeros((dec_seq, LANES), F32)
    sink_all = sink_ref[...] * LOG2E
    sink = jnp.concatenate([jnp.broadcast_to(sink_all[0:1, hd:hd + 1], (dec_seq, LANES)) for hd in range(n_h)], axis=0)

    for bb in range(nbb):
        r0 = bb * dec_seq
        q8 = q_ref[r0:r0 + dec_seq, :]
        pieces = []
        for hd in range(n_h):
            kvh = hd // n_g
            x = jnp.where(half8 == hd % 2, q8[:, (hd // 2) * LANES:(hd // 2 + 1) * LANES], 0.0)
            if hd % 2 != kvh % 2:
                x = pltpu.roll(x, HEAD, 1)
            pieces.append(jnp.concatenate([x if c == kvh // 2 else zero8 for c in range(nv)], axis=1))
        qbd = jnp.concatenate(pieces, axis=0).astype(BF16)
        kn = jnp.concatenate([kn_ref[r0:r0 + dec_seq, :], pad], axis=0).astype(BF16)
        vn = jnp.concatenate([vn_ref[r0:r0 + dec_seq, :], pad], axis=0).astype(BF16)
        s_c = _dot_nt(qbd, wk_ref[bb].astype(BF16)) + bias_c
        s_n = _dot_nt(qbd, kn) + bias_n
        m = jnp.maximum(jnp.max(jnp.maximum(s_c, s_n), axis=-1, keepdims=True), sink)
        p_c, p_n = jnp.exp2(s_c - m), jnp.exp2(s_n - m)
        den = jnp.sum(p_c + p_n, axis=-1, keepdims=True) + jnp.exp2(sink - m)
        o_all = (_dot(p_c.astype(BF16), wv_ref[bb].astype(BF16)) + _dot(p_n.astype(BF16), vn)) / _lane_tile(den, nv)
        for pr in range(n_h // 2):
            blks = []
            for e in range(2):
                hd = 2 * pr + e
                kvh = hd // n_g
                blk = o_all[hd * dec_seq:(hd + 1) * dec_seq, (kvh // 2) * LANES:(kvh // 2 + 1) * LANES]
                blks.append(pltpu.roll(blk, HEAD, 1) if kvh % 2 != e else blk)
            o_ref[r0:r0 + dec_seq, pr * LANES:(pr + 1) * LANES] = jnp.where(half8 == 0, blks[0], blks[1])


def _swa_sample(q_s, wk, wv, k_all, v_all, sinks, t_p, dec_batch, dec_seq):
    nbb = math.gcd(dec_batch, 8)
    rows = nbb * dec_seq
    rb0 = t_p // rows
    kw = k_all.shape[1]
    wb = wk.shape[1]
    in_specs = [pl.BlockSpec((rows, q_s.shape[1]), lambda i: (i, 0)),
                pl.BlockSpec((nbb, wb, kw), lambda i: (i, 0, 0)),
                pl.BlockSpec((nbb, wb, kw), lambda i: (i, 0, 0)),
                pl.BlockSpec((rows, kw), lambda i: (rb0 + i, 0)),
                pl.BlockSpec((rows, kw), lambda i: (rb0 + i, 0)),
                _const_spec(sinks.shape)]
    return pl.pallas_call(
        functools.partial(_swa_sample_kernel, dec_seq=dec_seq), grid=(dec_batch // nbb,), in_specs=in_specs,
        out_specs=pl.BlockSpec((rows, q_s.shape[1]), lambda i: (i, 0)),
        out_shape=jax.ShapeDtypeStruct(q_s.shape, F32),
        compiler_params=_params(("arbitrary",)), name="swa_sample")(q_s, wk, wv, k_all, v_all, sinks)


def _out_mlp_kernel(*refs, n_h, n_act, n_p, split_out):
    h_refs = refs[:n_h]
    pos = n_h
    act_refs = []
    for na in n_act:
        act_refs.append(refs[pos:pos + na])
        pos += na
    wo_ref, g_ref, wu_ref, wd_ref = refs[pos:pos + 4]
    pos += 4
    n_o = 2 if split_out else 1
    out_refs = refs[pos:pos + n_o]
    h1_ref, hn_ref, acc_ref = refs[pos + n_o:]
    i = pl.program_id(0)
    j = pl.program_id(1)
    d = h1_ref.shape[1]

    @pl.when(j == 0)
    def _():
        mix = None
        for s, ar in enumerate(act_refs):
            t = _dot(_row_load(ar, i, n_p, BF16), wo_ref[s * d:(s + 1) * d, :])
            mix = t if mix is None else mix + t
        h1 = _row_load(h_refs, i, n_p, F32) + mix
        h1_ref[...] = h1
        hn_ref[...] = (h1 * lax.rsqrt(jnp.mean(h1 * h1, axis=-1, keepdims=True) + EPS) * g_ref[...]).astype(BF16)
        acc_ref[...] = jnp.zeros_like(acc_ref)

    u = jnp.maximum(_dot(hn_ref[...], wu_ref[...]), 0.0)
    acc_ref[...] += _dot((u * u).astype(BF16), wd_ref[...])

    @pl.when(j == pl.num_programs(1) - 1)
    def _():
        res = h1_ref[...] + acc_ref[...]
        if split_out:
            @pl.when(i < n_p)
            def _():
                out_refs[0][...] = res

            @pl.when(i >= n_p)
            def _():
                out_refs[1][...] = res
        else:
            out_refs[0][...] = res


def _out_mlp(h_src, act_srcs, w_out, gain, w_up, w_down, tm, n_p, n_t, split_out):
    d = gain.shape[-1]
    d_ff = w_up.shape[1]
    tf = min(1024, d_ff)
    h_specs, h_args = _row_specs(h_src, tm, n_p)
    in_specs, args, n_act = list(h_specs), list(h_args), []
    for a in act_srcs:
        sp, ar = _row_specs(a, tm, n_p)
        in_specs += sp
        args += ar
        n_act.append(len(ar))
    in_specs += [_const_spec(w_out.shape), _const_spec((1, d)),
                 pl.BlockSpec((d, tf), lambda i, j: (0, j)), pl.BlockSpec((tf, d), lambda i, j: (j, 0))]
    args += [w_out, gain.reshape(1, d), w_up, w_down]
    if split_out:
        n_s = n_t - n_p
        out_shape = [jax.ShapeDtypeStruct((n_p * tm, d), F32), jax.ShapeDtypeStruct((n_s * tm, d), F32)]
        out_specs = [pl.BlockSpec((tm, d), lambda i, j: (jnp.minimum(i, n_p - 1), 0)),
                     pl.BlockSpec((tm, d), lambda i, j: (jnp.maximum(i - n_p, 0), 0))]
    else:
        out_shape = [jax.ShapeDtypeStruct((n_t * tm, d), F32)]
        out_specs = [pl.BlockSpec((tm, d), lambda i, j: (i, 0))]
    kern = functools.partial(_out_mlp_kernel, n_h=len(h_args), n_act=tuple(n_act), n_p=n_p, split_out=split_out)
    return pl.pallas_call(
        kern, grid=(n_t, d_ff // tf), in_specs=in_specs, out_specs=out_specs, out_shape=out_shape,
        scratch_shapes=[pltpu.VMEM((tm, d), F32), pltpu.VMEM((tm, d), BF16), pltpu.VMEM((tm, d), F32)],
        compiler_params=_params(("arbitrary", "arbitrary")), name="out_mlp")(*args)


def kernel(x_prompt, x_sample, cache_k, cache_v, page_table, state_conv, state_ssm, cache_win_k, cache_win_v,
           norm_mix, norm_mlp, w_up, w_down, a_w_in, a_conv_w, a_conv_b, a_dt_bias, a_A_log, a_D, a_ssd_norm,
           a_q_norm, a_k_norm, a_lam_q1, a_lam_k1, a_lam_q2, a_lam_k2, a_subln, a_w_out,
           c_w_in, c_q_norm, c_k_norm, c_sinks, c_w_out):
    batch, seq, d = x_prompt.shape
    dec_batch, dec_seq, _ = x_sample.shape
    page = cache_k.shape[2]
    past_len = page_table.shape[1] * page
    t_p, t_s = batch * seq, dec_batch * dec_seq
    assert a_w_in.shape[0] == 1 and c_w_in.shape[0] == 1 and norm_mix.shape[0] == 2, "kernel is written for depth 2"
    assert d % (2 * LANES) == 0 and seq % SSD_CHUNK == 0 and SSD_CHUNK % dec_seq == 0
    assert dec_batch % (SSD_CHUNK // dec_seq) == 0 and cache_win_k.shape[2] == WINDOW

    tm = next(t for t in (512, 256, 128) if t_p % t == 0 and t_s % t == 0 and seq % t == 0 and t % dec_seq == 0)
    n_p, n_t = t_p // tm, (t_p + t_s) // tm

    xp = x_prompt.reshape(t_p, d)
    xs = x_sample.reshape(t_s, d)
    rope = _rope_tables(seq, past_len, dec_seq, tm)
    ii = jnp.arange(LANES)
    bd = jnp.where((ii[:, None] // HEAD) == (ii[None, :] // HEAD), 1.0 / HEAD, 0.0).astype(BF16)

    xw = d
    cw = xw + 2 * SSD_GROUPS * SSD_STATE
    n_h = xw // SSD_HEAD_DIM
    hpg = n_h // SSD_GROUPS
    qw = d
    kvw = DIFF_KV_HEADS * 2 * DIFF_HEAD_DIM
    offs = [0, xw, xw + cw, xw + cw + n_h, xw + cw + n_h + qw, xw + cw + n_h + qw + kvw]
    w_in = a_w_in[0]
    w_z, w_xbc, w_dt, w_q, w_k, w_v = (w_in[:, offs[0]:offs[1]], w_in[:, offs[1]:offs[2]], w_in[:, offs[2]:offs[3]],
                                       w_in[:, offs[3]:offs[4]], w_in[:, offs[4]:offs[5]], w_in[:, offs[5]:])
    w_dt_g = jnp.pad(w_dt.reshape(d, SSD_GROUPS, hpg), ((0, 0), (0, 0), (0, LANES - hpg))).reshape(d, SSD_GROUPS * LANES)
    bf = lambda w: w.astype(BF16)
    segs0 = [
        dict(w=bf(w_z), kind="plain", width=xw, outs=[dict(dtype=BF16)]),
        dict(w=bf(w_xbc), kind="plain", width=cw, outs=[dict(dtype=F32)]),
        dict(w=bf(w_dt_g), kind="plain", width=SSD_GROUPS * LANES, outs=[dict(dtype=F32)]),
        dict(w=bf(w_q), kind="qk", width=qw, gain=a_q_norm[0], scale=DIFF_HEAD_DIM ** -0.5 * LOG2E, outs=[dict(dtype=BF16)]),
        dict(w=bf(w_k), kind="qk", width=kvw, gain=a_k_norm[0], scale=1.0,
             outs=[dict(dtype=F32, rows=DIFF_KV_HEADS), dict(dtype=BF16)]),
        dict(w=bf(w_v), kind="plain", width=kvw, outs=[dict(dtype=F32, rows=DIFF_KV_HEADS), dict(dtype=BF16)]),
    ]
    z, xbc, dt, q0, k_p, k_s, k0b, v_p, v_s, v0b = _norm_proj((xp, xs), norm_mix[0], segs0, rope, bd, tm, n_p, n_t)

    pad_l = lambda v: jnp.pad(v.astype(F32).reshape(SSD_GROUPS, hpg), ((0, 0), (0, LANES - hpg))).reshape(1, SSD_GROUPS * LANES)
    dtb, alog = pad_l(a_dt_bias[0]), pad_l(a_A_log[0])
    d_x = jnp.repeat(a_D[0].astype(F32), SSD_HEAD_DIM).reshape(1, xw)
    nw = a_ssd_norm[0].astype(F32).reshape(1, xw)
    gw = xw // SSD_GROUPS
    eexp = jnp.where(ii[:, None] == (jnp.arange(gw)[None, :] // SSD_HEAD_DIM), 1.0, 0.0).astype(BF16)
    conv_w, conv_b = a_conv_w[0].astype(F32), a_conv_b[0].astype(F32).reshape(1, cw)

    y_p, p_ssm = _ssd_prompt(xbc, dt, z, conv_w, conv_b, dtb, alog, d_x, nw, eexp, batch, seq)
    prev8 = jnp.pad(state_conv[0].astype(F32), ((0, 0), (8 - (SSD_CONV - 1), 0), (0, 0))).reshape(dec_batch * 8, cw)
    assert dec_seq == 8, "sample conv halo layout assumes 8-token sequences"
    y_s, s_ssm = _ssd_sample(xbc, prev8, dt, z, conv_w, conv_b, dtb, alog, d_x, nw, eexp,
                             state_ssm[0].astype(F32), t_p, dec_batch, dec_seq)

    lam_init0 = 0.8 - 0.6 * math.exp(-0.3 * 0)
    lam_vecs = [v[0].astype(F32).reshape(1, DIFF_HEAD_DIM) for v in (a_lam_q1, a_lam_k1, a_lam_q2, a_lam_k2)]
    subln = a_subln[0].astype(F32).reshape(1, LANES)
    o_p = _diff_prompt(q0, k0b, v0b, lam_vecs, subln, batch, seq, lam_init0)
    n_pool = cache_k.shape[1]
    w_out0, w_up0, w_down0 = bf(a_w_out[0]), bf(w_up[0]), bf(w_down[0])
    w_out1, w_up1, w_down1 = bf(c_w_out[0]), bf(w_up[1]), bf(w_down[1])
    ck = cache_k[0].reshape(n_pool, page * DIFF_KV_HEADS, LANES)
    cv = cache_v[0].reshape(n_pool, page * DIFF_KV_HEADS, LANES)
    q0_s = q0[t_p:].astype(F32)
    pt = page_table.astype(jnp.int32)
    n_parts = 2 if page_table.shape[1] % 2 == 0 else 1
    n_s = n_t - n_p
    _, state, h1_p = _diff_paged(q0_s, k_s, v_s, ck, cv, pt, lam_vecs, subln, dec_batch, dec_seq, lam_init0,
                                 None, 0, n_parts, xp, [y_p, o_p], w_out0, norm_mlp[0], w_up0, w_down0, tm)

    cqw = d
    ckw = SWA_KV_HEADS * SWA_HEAD_DIM
    wc = c_w_in[0]
    segs1 = [
        dict(w=bf(wc[:, :cqw]), kind="qk", width=cqw, gain=c_q_norm[0], scale=SWA_HEAD_DIM ** -0.5 * LOG2E, outs=[dict(dtype=BF16)]),
        dict(w=bf(wc[:, cqw:cqw + ckw]), kind="qk", width=ckw, gain=c_k_norm[0], scale=1.0,
             outs=[dict(dtype=F32), dict(dtype=BF16, dup=True)]),
        dict(w=bf(wc[:, cqw + ckw:]), kind="dupv", width=ckw, outs=[dict(dtype=F32), dict(dtype=BF16, dup=True)]),
    ]
    sinks = c_sinks[0].astype(F32).reshape(1, -1)
    q1_p, k1_p, k1d_p, v1_p, v1d_p = _norm_proj(h1_p, norm_mix[1], segs1, rope, bd, tm, n_p, n_p)
    o1_p = _swa_prompt(q1_p, k1d_p, v1d_p, sinks, batch, seq)
    if n_parts == 2:
        o_s, _, y_prompt = _diff_paged(q0_s, k_s, v_s, ck, cv, pt, lam_vecs, subln, dec_batch, dec_seq, lam_init0,
                                       state, 1, n_parts, h1_p, [o1_p], w_out1, norm_mlp[1], w_up1, w_down1, tm)
    else:
        o_s = _
        (y_prompt,) = _out_mlp(h1_p, [o1_p], w_out1, norm_mlp[1], w_up1, w_down1, tm, n_p, n_p, split_out=False)

    (h1_s,) = _out_mlp(xs, [y_s, o_s], w_out0, norm_mlp[0], w_up0, w_down0, tm, n_s, n_s, split_out=False)
    q1_s, k1_s, _, v1_s, _ = _norm_proj(h1_s, norm_mix[1], segs1, rope, bd, tm, 0, n_s)
    o1_s = _swa_sample(q1_s.astype(F32), cache_win_k[0].reshape(dec_batch, WINDOW, ckw),
                       cache_win_v[0].reshape(dec_batch, WINDOW, ckw), k1_s, v1_s, sinks, 0, dec_batch, dec_seq)
    (y_sample,) = _out_mlp(h1_s, [o1_s], w_out1, norm_mlp[1], w_up1, w_down1, tm, n_s, n_s, split_out=False)

    n_kv = DIFF_KV_HEADS
    p_k = k_p.reshape(1, batch, seq, n_kv, kvw // n_kv)
    p_v = v_p.reshape(1, batch, seq, n_kv, kvw // n_kv)
    s_k = k_s.reshape(1, dec_batch, dec_seq, n_kv, kvw // n_kv)
    s_v = v_s.reshape(1, dec_batch, dec_seq, n_kv, kvw // n_kv)
    keep = SSD_CONV - 1
    tail = lambda a, n: jnp.stack([a[(b + 1) * seq - n:(b + 1) * seq] for b in range(batch)])
    p_conv = tail(xbc, keep)[None]
    s_conv = jnp.concatenate([state_conv[0].astype(F32), xbc[t_p:].reshape(dec_batch, dec_seq, cw)], axis=1)[:, -keep:][None]
    wk_p = min(WINDOW, seq)
    p_wk = tail(k1_p, wk_p).reshape(1, batch, wk_p, SWA_KV_HEADS, SWA_HEAD_DIM)
    p_wv = tail(v1_p, wk_p).reshape(1, batch, wk_p, SWA_KV_HEADS, SWA_HEAD_DIM)
    k1s = k1_s.reshape(dec_batch, dec_seq, SWA_KV_HEADS, SWA_HEAD_DIM)
    v1s = v1_s.reshape(dec_batch, dec_seq, SWA_KV_HEADS, SWA_HEAD_DIM)
    wb = cache_win_k.shape[2]
    s_wk = jnp.concatenate([cache_win_k[0].astype(F32), k1s], axis=1)[:, -wb:][None]
    s_wv = jnp.concatenate([cache_win_v[0].astype(F32), v1s], axis=1)[:, -wb:][None]
    return (y_prompt.reshape(batch, seq, d), y_sample.reshape(dec_batch, dec_seq, d),
            p_k, p_v, s_k, s_v, p_conv, s_conv, p_ssm[None], s_ssm[None], p_wk, p_wv, s_wk, s_wv)
```
